```python
import math
import jax, jax.numpy as jnp
from jax import lax
import numpy as np

D_MODEL = 4096
BATCH = 8
SEQ = 2048
DEPTH = 2

GRID_W = 64
CTX_LEN = 256
N_MIXERS = 2
N_ATTN_LAYERS = (DEPTH + 1) // 2
N_CONV_LAYERS = DEPTH // 2
DIFF_HEAD_DIM = 128
DIFF_HEADS = D_MODEL // (2 * DIFF_HEAD_DIM)
V_HEAD_DIM = 2 * DIFF_HEAD_DIM
ATTN_SCALE = DIFF_HEAD_DIM ** -0.5
ROPE_DIM_PER_AXIS = DIFF_HEAD_DIM // 2
ROPE_BASE = 10000.0
CONV_WIDTH = 31
CONV_PAD = CONV_WIDTH // 2
D_FF = 4 * D_MODEL
Q_BLOCK = 128
RMS_EPS = 1e-6
LN_EPS = 1e-5

kernel_name = "hybrid_diffattn_conformer_dit_block"


def rms_norm(x, g):
    xf = x.astype(jnp.float32)
    y = xf * lax.rsqrt(jnp.mean(xf * xf, axis=-1, keepdims=True) + RMS_EPS)
    return (y * g.astype(jnp.float32)).astype(x.dtype)


def layer_norm(x, g, b):
    xf = x.astype(jnp.float32)
    mu = jnp.mean(xf, axis=-1, keepdims=True)
    var = jnp.mean(jnp.square(xf - mu), axis=-1, keepdims=True)
    y = (xf - mu) * lax.rsqrt(var + LN_EPS)
    return (y * g.astype(jnp.float32) + b.astype(jnp.float32)).astype(x.dtype)


def adaln(cvec, w, b):
    m = jax.nn.silu(cvec) @ w + b
    m = m.reshape(m.shape[:-1] + (1, 6, D_MODEL))
    return [m[..., j, :] for j in range(6)]


def modulate(h, shift, scale):
    return h * (1.0 + scale) + shift


def axial_rope_tables(length, dtype):
    rows = length // GRID_W
    row = jnp.repeat(jnp.arange(rows, dtype=jnp.float32), GRID_W)
    col = jnp.tile(jnp.arange(GRID_W, dtype=jnp.float32), rows)
    n = ROPE_DIM_PER_AXIS // 2
    inv = ROPE_BASE ** (-jnp.arange(n, dtype=jnp.float32) / n)
    ar = row[:, None] * inv
    ac = col[:, None] * inv
    ang = jnp.concatenate([ar, ar, ac, ac], axis=-1)
    return (jnp.cos(ang)[None, :, None, :].astype(dtype),
            jnp.sin(ang)[None, :, None, :].astype(dtype))


def apply_rope(x, cos, sin):
    h = ROPE_DIM_PER_AXIS // 2
    xr1, xr2 = x[..., :h], x[..., h:2 * h]
    xc1, xc2 = x[..., 2 * h:3 * h], x[..., 3 * h:]
    rot = jnp.concatenate([-xr2, xr1, -xc2, xc1], axis=-1)
    return x * cos + rot * sin


def diff_core(q, k, v, lam):
    s = jnp.einsum('bqhcd,bkhcd->bchqk', q, k).astype(jnp.float32) * ATTN_SCALE
    p = jax.nn.softmax(s, axis=-1)
    a = p[:, 0] - lam * p[:, 1]
    return jnp.einsum('bhqk,bkhe->bqhe', a.astype(v.dtype), v)


def diff_head_out(o, subln_g, lam_init, w_o):
    o = rms_norm(o, subln_g) * (1.0 - lam_init)
    return o.reshape(o.shape[:2] + (D_MODEL,)) @ w_o


def diff_attention(h_lat, h_ctx, w_qkv, w_o, lq1, lk1, lq2, lk2, subln_g, layer_idx, ctx_queries):
    B, L, D = h_lat.shape
    C = h_ctx.shape[1]
    H, dk, dv = DIFF_HEADS, DIFF_HEAD_DIM, V_HEAD_DIM
    lam_init = 0.8 - 0.6 * math.exp(-0.3 * layer_idx)
    f32 = jnp.float32
    lam = (jnp.exp(jnp.sum(lq1.astype(f32) * lk1.astype(f32)))
           - jnp.exp(jnp.sum(lq2.astype(f32) * lk2.astype(f32))) + lam_init)
    q, k, v = jnp.split(h_lat @ w_qkv, 3, axis=-1)
    cos, sin = axial_rope_tables(L, h_lat.dtype)
    q = apply_rope(q.reshape(B, L, 2 * H, dk), cos, sin).reshape(B, L, H, 2, dk)
    k = apply_rope(k.reshape(B, L, 2 * H, dk), cos, sin).reshape(B, L, H, 2, dk)
    v = v.reshape(B, L, H, dv)
    if ctx_queries:
        qc, kc, vc = jnp.split(h_ctx @ w_qkv, 3, axis=-1)
        qc = qc.reshape(B, C, H, 2, dk)
    else:
        kc, vc = jnp.split(h_ctx @ w_qkv[:, D:], 2, axis=-1)
    kc = kc.reshape(B, C, H, 2, dk)
    vc = vc.reshape(B, C, H, dv)
    k_all = jnp.concatenate([kc, k], axis=1)
    v_all = jnp.concatenate([vc, v], axis=1)
    nblk = L // Q_BLOCK
    q_blocks = q.reshape(B, nblk, Q_BLOCK, H, 2, dk).swapaxes(0, 1)
    o = lax.map(lambda qb: diff_core(qb, k_all, v_all, lam), q_blocks)
    o = o.swapaxes(0, 1).reshape(B, L, H, dv)
    y_lat = diff_head_out(o, subln_g, lam_init, w_o)
    y_ctx = None
    if ctx_queries:
        y_ctx = diff_head_out(diff_core(qc, kc, vc, lam), subln_g, lam_init, w_o)
    return y_lat, y_ctx


def conformer_conv(h, pw1_w, pw1_b, dw_w, dw_b, ln_g, ln_b, pw2_w, pw2_b):
    a, g = jnp.split(h @ pw1_w + pw1_b, 2, axis=-1)
    u = a * jax.nn.sigmoid(g)
    u = lax.conv_general_dilated(u, dw_w[:, None, :], window_strides=(1,),
                                 padding=((CONV_PAD, CONV_PAD),),
                                 dimension_numbers=('NWC', 'WIO', 'NWC'),
                                 feature_group_count=D_MODEL) + dw_b
    u = jax.nn.silu(layer_norm(u, ln_g, ln_b))
    return u @ pw2_w + pw2_b


def sq_relu_mlp(h, w1, w2):
    return jnp.square(jax.nn.relu(h @ w1)) @ w2


def setup_inputs(seed: int = 0) -> dict:
    key = jax.random.key(seed)
    ks = jax.random.split(key, 26)
    D = D_MODEL
    na, nc = N_ATTN_LAYERS, N_CONV_LAYERS

    def nrm(k, shape, s):
        return jax.random.normal(k, shape, jnp.float32) * s

    return {
        "x": nrm(ks[0], (BATCH, SEQ, D), 1.0),
        "c": nrm(ks[1], (BATCH, D), 1.0),
        "ctx": nrm(ks[2], (BATCH, CTX_LEN, D), 1.0),
        "c_ctx": nrm(ks[3], (D,), 1.0),
        "ada_w": nrm(ks[4], (DEPTH, D, 6 * D), D ** -0.5),
        "ada_b": nrm(ks[5], (DEPTH, 6 * D), 0.01),
        "norm_mix_g": 1.0 + nrm(ks[6], (DEPTH, D), 0.01),
        "norm_mlp_g": 1.0 + nrm(ks[7], (DEPTH, D), 0.01),
        "norm_final_g": 1.0 + nrm(ks[8], (D,), 0.01),
        "attn_w_qkv": nrm(ks[9], (na, D, 3 * D), D ** -0.5),
        "attn_w_o": nrm(ks[10], (na, D, D), D ** -0.5),
        "lambda_q1": nrm(ks[11], (na, DIFF_HEAD_DIM), 0.1),
        "lambda_k1": nrm(ks[12], (na, DIFF_HEAD_DIM), 0.1),
        "lambda_q2": nrm(ks[13], (na, DIFF_HEAD_DIM), 0.1),
        "lambda_k2": nrm(ks[14], (na, DIFF_HEAD_DIM), 0.1),
        "attn_subln_g": 1.0 + nrm(ks[15], (na, V_HEAD_DIM), 0.01),
        "conv_pw1_w": nrm(ks[16], (nc, D, 2 * D), D ** -0.5),
        "conv_pw1_b": nrm(ks[17], (nc, 2 * D), 0.01),
        "conv_dw_w": nrm(ks[18], (nc, CONV_WIDTH, D), CONV_WIDTH ** -0.5),
        "conv_dw_b": nrm(ks[19], (nc, D), 0.01),
        "conv_ln_g": 1.0 + nrm(ks[20], (nc, D), 0.01),
        "conv_ln_b": nrm(ks[21], (nc, D), 0.01),
        "conv_pw2_w": nrm(ks[22], (nc, D, D), D ** -0.5),
        "conv_pw2_b": nrm(ks[23], (nc, D), 0.01),
        "mlp_w1": nrm(ks[24], (DEPTH, D, D_FF), D ** -0.5),
        "mlp_w2": nrm(ks[25], (DEPTH, D_FF, D), D_FF ** -0.5),
    }


def reference(x, c, ctx, c_ctx, ada_w, ada_b, norm_mix_g, norm_mlp_g, norm_final_g,
              attn_w_qkv, attn_w_o, lambda_q1, lambda_k1, lambda_q2, lambda_k2, attn_subln_g,
              conv_pw1_w, conv_pw1_b, conv_dw_w, conv_dw_b, conv_ln_g, conv_ln_b, conv_pw2_w, conv_pw2_b,
              mlp_w1, mlp_w2):
    x_lat, x_ctx = x, ctx
    for i in range(DEPTH):
        kind = i % N_MIXERS
        idx = i // N_MIXERS
        ctx_update = any(j % N_MIXERS == 0 for j in range(i + 1, DEPTH))
        ctx_in = (kind == 0) or ctx_update
        sh1, sc1, g1, sh2, sc2, g2 = adaln(c, ada_w[i], ada_b[i])
        h_lat = modulate(rms_norm(x_lat, norm_mix_g[i]), sh1, sc1)
        if ctx_in:
            csh1, csc1, cg1, csh2, csc2, cg2 = adaln(c_ctx, ada_w[i], ada_b[i])
            h_ctx = modulate(rms_norm(x_ctx, norm_mix_g[i]), csh1, csc1)
        if kind == 0:
            y_lat, y_ctx = diff_attention(h_lat, h_ctx, attn_w_qkv[idx], attn_w_o[idx],
                                          lambda_q1[idx], lambda_k1[idx], lambda_q2[idx], lambda_k2[idx],
                                          attn_subln_g[idx], i, ctx_update)
        else:
            conv_p = (conv_pw1_w[idx], conv_pw1_b[idx], conv_dw_w[idx], conv_dw_b[idx],
                      conv_ln_g[idx], conv_ln_b[idx], conv_pw2_w[idx], conv_pw2_b[idx])
            y_lat = conformer_conv(h_lat, *conv_p)
            y_ctx = conformer_conv(h_ctx, *conv_p) if ctx_update else None
        x_lat = x_lat + g1 * y_lat
        x_lat = x_lat + g2 * sq_relu_mlp(modulate(rms_norm(x_lat, norm_mlp_g[i]), sh2, sc2),
                                         mlp_w1[i], mlp_w2[i])
        if ctx_update:
            x_ctx = x_ctx + cg1 * y_ctx
            x_ctx = x_ctx + cg2 * sq_relu_mlp(modulate(rms_norm(x_ctx, norm_mlp_g[i]), csh2, csc2),
                                              mlp_w1[i], mlp_w2[i])
    return rms_norm(x_lat, norm_final_g)
```

```python
import functools
import math

import jax
import jax.numpy as jnp
from jax import lax
from jax.experimental import pallas as pl
from jax.experimental.pallas import tpu as pltpu

GRID_W = 64
DIFF_HEAD_DIM = 128
V_HEAD_DIM = 2 * DIFF_HEAD_DIM
ATTN_SCALE = DIFF_HEAD_DIM ** -0.5
ROPE_BASE = 10000.0
CONV_WIDTH = 31
CONV_PAD = CONV_WIDTH // 2
RMS_EPS = 1e-6
LN_EPS = 1e-5
N_MIXERS = 2

V7X_LANES = 128
V7X_SUBLANES = 8
V7X_VMEM_LIMIT_CAP = 60000 * 1024

MM_BM = 1024
MM_BN = 1024
MLP_DOWN_BK = 2048
GLU_BN = 512

F32 = jnp.float32
BF16 = jnp.bfloat16


def _nbytes(shape, dtype):
    return math.prod(shape) * jnp.dtype(dtype).itemsize


def _vmem_limit(pipelined_bytes, resident_bytes):
    return int(min(V7X_VMEM_LIMIT_CAP, 2 * pipelined_bytes + resident_bytes))


def _params(n_axes, vmem_bytes):
    return pltpu.CompilerParams(
        dimension_semantics=("arbitrary",) * n_axes, vmem_limit_bytes=vmem_bytes)


def _adaln_kernel(c_ref, w_ref, b_ref, o_ref):
    c = c_ref[...]
    s = (c * jax.nn.sigmoid(c)).astype(BF16)
    acc = jnp.dot(s, w_ref[...].astype(BF16), preferred_element_type=F32)
    o_ref[...] = acc + b_ref[...]


def _adaln(cvec, ada_w, ada_b, *, bn=512):
    depth, d, n = ada_w.shape
    rows = cvec.shape[0]
    pipelined = _nbytes((d, bn), F32) + _nbytes((rows, bn), F32) + _nbytes((1, bn), F32)
    resident = 2 * _nbytes((rows, d), F32) + _nbytes((d, bn), BF16)
    return pl.pallas_call(
        _adaln_kernel,
        out_shape=jax.ShapeDtypeStruct((depth, rows, n), F32),
        grid=(depth, n // bn),
        in_specs=[
            pl.BlockSpec((rows, d), lambda l, j: (0, 0)),
            pl.BlockSpec((None, d, bn), lambda l, j: (l, 0, j)),
            pl.BlockSpec((None, 1, bn), lambda l, j: (l, 0, j)),
        ],
        out_specs=pl.BlockSpec((None, rows, bn), lambda l, j: (l, 0, j)),
        compiler_params=_params(2, _vmem_limit(pipelined, resident)),
        name="adaln",
    )(cvec, ada_w, ada_b.reshape(depth, 1, n))


def _norm_kernel(x_ref, g_ref, *rest, modulate):
    o_ref = rest[-1]
    x = x_ref[...]
    y = x * lax.rsqrt(jnp.mean(x * x, axis=-1, keepdims=True) + RMS_EPS)
    y = y * g_ref[...]
    if modulate:
        sh_ref, sc_ref = rest[0], rest[1]
        y = y * (1.0 + sc_ref[...]) + sh_ref[...]
    o_ref[...] = y.astype(o_ref.dtype)


def _norm(x, g, shift=None, scale=None, *, out_dtype, tr=256):
    b, t, d = x.shape
    modulate = shift is not None
    operands = [x, g.reshape(1, d)]
    in_specs = [
        pl.BlockSpec((None, tr, d), lambda bi, ti: (bi, ti, 0)),
        pl.BlockSpec((1, d), lambda bi, ti: (0, 0)),
    ]
    if modulate:
        per_batch = shift.shape[0] == b
        vec_map = (lambda bi, ti: (bi, 0, 0)) if per_batch else (lambda bi, ti: (0, 0, 0))
        operands += [shift, scale]
        in_specs += [pl.BlockSpec((None, 1, d), vec_map)] * 2
    pipelined = _nbytes((tr, d), F32) + _nbytes((tr, d), out_dtype) + 3 * _nbytes((1, d), F32)
    resident = 3 * _nbytes((tr, d), F32)
    return pl.pallas_call(
        functools.partial(_norm_kernel, modulate=modulate),
        out_shape=jax.ShapeDtypeStruct((b, t, d), out_dtype),
        grid=(b, t // tr),
        in_specs=in_specs,
        out_specs=pl.BlockSpec((None, tr, d), lambda bi, ti: (bi, ti, 0)),
        compiler_params=_params(2, _vmem_limit(pipelined, resident)),
        name="rmsnorm_mod" if modulate else "rmsnorm",
    )(*operands)


def _finish_over_k(part, acc_ref, nk, finish):
    if nk == 1:
        finish(part)
        return
    k = pl.program_id(2)

    @pl.when(k == 0)
    def _():
        acc_ref[...] = part

    @pl.when(jnp.logical_and(k > 0, k < nk - 1))
    def _():
        acc_ref[...] += part

    @pl.when(k == nk - 1)
    def _():
        finish(acc_ref[...] + part)


def _dot(a_ref, w_ref):
    return jnp.dot(a_ref[...], w_ref[...], preferred_element_type=F32)


def _mm_plain_kernel(a_ref, w_ref, o_ref, *scratch, nk):
    def finish(acc):
        o_ref[...] = acc.astype(o_ref.dtype)

    _finish_over_k(_dot(a_ref, w_ref), scratch[0] if scratch else None, nk, finish)


def _mm_rope_kernel(a_ref, w_ref, cos_ref, sin_lo_ref, sin_hi_ref, o_ref, *, rope_cols, bn):
    acc = _dot(a_ref, w_ref)
    j = pl.program_id(1)
    quarter = DIFF_HEAD_DIM // 4

    @pl.when(j * bn < rope_cols)
    def _():
        cos, sin_lo, sin_hi = cos_ref[...], sin_lo_ref[...], sin_hi_ref[...]
        for c0 in range(0, bn, DIFF_HEAD_DIM):
            xh = acc[:, c0:c0 + DIFF_HEAD_DIM]
            r = (xh * cos
                 + pltpu.roll(xh, DIFF_HEAD_DIM - quarter, 1) * sin_lo
                 + pltpu.roll(xh, quarter, 1) * sin_hi)
            o_ref[:, c0:c0 + DIFF_HEAD_DIM] = r.astype(o_ref.dtype)

    @pl.when(j * bn >= rope_cols)
    def _():
        o_ref[...] = acc.astype(o_ref.dtype)


def _mm_relu2_kernel(a_ref, w_ref, o_ref):
    acc = _dot(a_ref, w_ref)
    r = jnp.maximum(acc, 0.0)
    o_ref[...] = (r * r).astype(o_ref.dtype)


def _mm_glu_kernel(a_ref, wa_ref, wg_ref, ba_ref, bg_ref, o_ref):
    lin = _dot(a_ref, wa_ref) + ba_ref[...]
    gate = _dot(a_ref, wg_ref) + bg_ref[...]
    o_ref[...] = (lin * jax.nn.sigmoid(gate)).astype(o_ref.dtype)


def _mm_resid_kernel(a_ref, w_ref, x_ref, gate_ref, *rest, nk, has_bias):
    if has_bias:
        bias_ref, o_ref, scratch = rest[0], rest[1], rest[2:]
    else:
        bias_ref, o_ref, scratch = None, rest[0], rest[1:]

    def finish(acc):
        y = acc + bias_ref[...] if has_bias else acc
        o_ref[...] = x_ref[...] + gate_ref[...] * y

    _finish_over_k(_dot(a_ref, w_ref), scratch[0] if scratch else None, nk, finish)


def _mm_call(body, a, w, n_cols, *, w_col_offsets=(0,), extras=(), out_dtype, bm, bn, bk=None,
             extra_tile_bytes=0, n_temps=3, name):
    m, kdim = a.shape
    bk = kdim if bk is None else bk
    nk = kdim // bk
    assert m % bm == 0 and n_cols % bn == 0 and kdim % bk == 0, (name, a.shape, n_cols, bm, bn, bk)
    in_specs = [pl.BlockSpec((bm, bk), lambda i, j, k: (i, k))]
    for off in w_col_offsets:
        assert off % bn == 0
        in_specs.append(pl.BlockSpec((bk, bn), lambda i, j, k, ob=off // bn: (k, j + ob)))
    in_specs += [spec for _, spec in extras]
    scratch = [pltpu.VMEM((bm, bn), F32)] if nk > 1 else []
    pipelined = (_nbytes((bm, bk), BF16) + len(w_col_offsets) * _nbytes((bk, bn), BF16)
                 + _nbytes((bm, bn), out_dtype) + extra_tile_bytes)
    resident = (n_temps + len(scratch)) * _nbytes((bm, bn), F32)
    return pl.pallas_call(
        body,
        out_shape=jax.ShapeDtypeStruct((m, n_cols), out_dtype),
        grid=(m // bm, n_cols // bn, nk),
        in_specs=in_specs,
        out_specs=pl.BlockSpec((bm, bn), lambda i, j, k: (i, j)),
        scratch_shapes=scratch,
        compiler_params=_params(3, _vmem_limit(pipelined, resident)),
        name=name,
    )(a, *([w] * len(w_col_offsets)), *[arr for arr, _ in extras])


def _mm_resid(a, w, x, gate, seq_len, *, bias=None, bk=None, name):
    n = w.shape[1]
    bm, bn = min(MM_BM, seq_len), min(MM_BN, n)
    bk = a.shape[1] if bk is None else bk
    nk = a.shape[1] // bk
    blocks_per_batch = seq_len // bm
    extras = [
        (x, pl.BlockSpec((bm, bn), lambda i, j, k: (i, j))),
        (gate, pl.BlockSpec((None, 1, bn), lambda i, j, k: (i // blocks_per_batch, 0, j))),
    ]
    if bias is not None:
        extras.append((bias.reshape(1, n), pl.BlockSpec((1, bn), lambda i, j, k: (0, j))))
    return _mm_call(
        functools.partial(_mm_resid_kernel, nk=nk, has_bias=bias is not None),
        a, w, n, extras=extras, out_dtype=F32, bm=bm, bn=bn, bk=bk,
        extra_tile_bytes=_nbytes((bm, bn), F32), name=name)


def _attn_kernel(lam_vecs_ref, subln_g_ref, q_ref, kc_ref, k_ref, vc_ref, v_ref, o_ref,
                 s_ref, lam_ref, *, lam_init, key_chunk):
    dk = DIFF_HEAD_DIM
    n_ctx, n_lat = kc_ref.shape[0], k_ref.shape[0]
    bq = q_ref.shape[0]

    first = jnp.logical_and(pl.program_id(0) == 0,
                            jnp.logical_and(pl.program_id(1) == 0, pl.program_id(2) == 0))

    @pl.when(first)
    def _():
        lv = lam_vecs_ref[...]
        d1 = jnp.sum(lv[0:1, :] * lv[1:2, :], axis=-1, keepdims=True)
        d2 = jnp.sum(lv[2:3, :] * lv[3:4, :], axis=-1, keepdims=True)
        lam_ref[...] = jnp.broadcast_to(jnp.exp(d1) - jnp.exp(d2) + lam_init, lam_ref.shape)

    chunks = [(kc_ref, vc_ref, 0, n_ctx, 0)]
    chunks += [(k_ref, v_ref, r0, key_chunk, n_ctx + r0) for r0 in range(0, n_lat, key_chunk)]
    exp2_scale = ATTN_SCALE * math.log2(math.e)

    heads = []
    for comp in range(2):
        qc = q_ref[:, comp * dk:(comp + 1) * dk]
        m = None
        for key_ref, _, r0, rows, col in chunks:
            kk = key_ref[r0:r0 + rows, comp * dk:(comp + 1) * dk]
            s = lax.dot_general(qc, kk, (((1,), (1,)), ((), ())), preferred_element_type=F32)
            s_ref[comp, :, col:col + rows] = s
            cm = jnp.max(s, axis=-1, keepdims=True)
            m = cm if m is None else jnp.maximum(m, cm)
        denom = jnp.zeros((bq, 1), F32)
        acc = jnp.zeros((bq, V_HEAD_DIM), F32)
        for _, val_ref, r0, rows, col in chunks:
            e = jnp.exp2((s_ref[comp, :, col:col + rows] - m) * exp2_scale)
            denom = denom + jnp.sum(e, axis=-1, keepdims=True)
            acc = acc + jnp.dot(e.astype(BF16), val_ref[r0:r0 + rows, :], preferred_element_type=F32)
        heads.append(acc / denom)

    lam = lam_ref[0:1, 0:1]
    o = heads[0] - lam * heads[1]
    o = o * lax.rsqrt(jnp.mean(o * o, axis=-1, keepdims=True) + RMS_EPS)
    o = (o * subln_g_ref[...]) * (1.0 - lam_init)
    o_ref[...] = o.astype(o_ref.dtype)


def _diff_attention(qkv, kv_ctx, lam_vecs, subln_g, *, lam_init, n_heads, bq=512, key_chunk=512):
    b, l, d3 = qkv.shape
    d = d3 // 3
    c = kv_ctx.shape[1]
    hw = V_HEAD_DIM
    assert hw == 2 * DIFF_HEAD_DIM and n_heads * hw == d
    hb = d // hw
    pipelined = (2 * _nbytes((bq, hw), BF16) + 2 * _nbytes((c, hw), BF16) + 2 * _nbytes((l, hw), BF16))
    resident = (2 * _nbytes((bq, c + l), F32)
                + 4 * _nbytes((bq, key_chunk), F32)
                + 4 * _nbytes((bq, hw), F32))
    return pl.pallas_call(
        functools.partial(_attn_kernel, lam_init=lam_init, key_chunk=key_chunk),
        out_shape=jax.ShapeDtypeStruct((b, l, d), BF16),
        grid=(b, n_heads, l // bq),
        in_specs=[
            pl.BlockSpec((4, DIFF_HEAD_DIM), lambda bi, h, qi: (0, 0)),
            pl.BlockSpec((1, hw), lambda bi, h, qi: (0, 0)),
            pl.BlockSpec((None, bq, hw), lambda bi, h, qi: (bi, qi, h)),
            pl.BlockSpec((None, c, hw), lambda bi, h, qi: (bi, 0, h)),
            pl.BlockSpec((None, l, hw), lambda bi, h, qi: (bi, 0, hb + h)),
            pl.BlockSpec((None, c, hw), lambda bi, h, qi: (bi, 0, hb + h)),
            pl.BlockSpec((None, l, hw), lambda bi, h, qi: (bi, 0, 2 * hb + h)),
        ],
        out_specs=pl.BlockSpec((None, bq, hw), lambda bi, h, qi: (bi, qi, h)),
        scratch_shapes=[pltpu.VMEM((2, bq, c + l), F32),
                        pltpu.VMEM((V7X_SUBLANES, V7X_LANES), F32)],
        compiler_params=_params(3, _vmem_limit(pipelined, resident)),
        name="diff_attention",
    )(lam_vecs, subln_g.reshape(1, hw), qkv, kv_ctx, qkv, kv_ctx, qkv)


def _conv_ln_kernel(prev_ref, u_ref, next_ref, w_ref, dwb_ref, lng_ref, lnb_ref, o_ref,
                    ubuf, ybuf, *, halo, rb, cb):
    tr, d = u_ref.shape
    ti, nt = pl.program_id(1), pl.num_programs(1)
    ubuf[0:halo, :] = jnp.where(ti > 0, prev_ref[...], 0.0)
    ubuf[halo:halo + tr, :] = u_ref[...]
    ubuf[halo + tr:, :] = jnp.where(ti < nt - 1, next_ref[...], 0.0)

    def conv_chunk(ci, carry):
        c0 = pl.multiple_of(ci * cb, cb)
        for r0 in range(0, tr, rb):
            acc = jnp.zeros((rb, cb), F32)
            for tap in range(CONV_WIDTH):
                row = halo + r0 + tap - CONV_PAD
                acc = acc + ubuf[row:row + rb, pl.ds(c0, cb)] * w_ref[tap:tap + 1, pl.ds(c0, cb)]
            ybuf[r0:r0 + rb, pl.ds(c0, cb)] = acc + dwb_ref[:, pl.ds(c0, cb)]
        return carry

    lax.fori_loop(0, d // cb, conv_chunk, 0)

    for r0 in range(0, tr, rb):
        y = ybuf[r0:r0 + rb, :]
        mu = jnp.mean(y, axis=-1, keepdims=True)
        yc = y - mu
        var = jnp.mean(yc * yc, axis=-1, keepdims=True)
        z = yc * lax.rsqrt(var + LN_EPS) * lng_ref[...] + lnb_ref[...]
        o_ref[r0:r0 + rb, :] = (z * jax.nn.sigmoid(z)).astype(o_ref.dtype)


def _conv_ln_swish(u, dw_w, dw_b, ln_g, ln_b, *, tr=256, halo=16, rb=64, cb=512):
    b, l, d = u.shape
    assert halo >= CONV_PAD and tr % halo == 0 and l % tr == 0
    hb = tr // halo
    last_halo_block = l // halo - 1
    w_rows = -(-CONV_WIDTH // V7X_SUBLANES) * V7X_SUBLANES
    w_pad = jnp.zeros((w_rows, d), F32).at[:CONV_WIDTH].set(dw_w)
    vec = lambda v: v.reshape(1, d)
    vec_spec = pl.BlockSpec((1, d), lambda bi, ti: (0, 0))
    pipelined = (_nbytes((tr, d), F32) + 2 * _nbytes((halo, d), F32) + _nbytes((tr, d), BF16)
                 + _nbytes((w_rows + 3, d), F32))
    resident = _nbytes((tr + 2 * halo, d), F32) + _nbytes((tr, d), F32) + 8 * _nbytes((rb, d), F32)
    return pl.pallas_call(
        functools.partial(_conv_ln_kernel, halo=halo, rb=rb, cb=cb),
        out_shape=jax.ShapeDtypeStruct((b, l, d), BF16),
        grid=(b, l // tr),
        in_specs=[
            pl.BlockSpec((None, halo, d), lambda bi, ti: (bi, jnp.maximum(ti * hb - 1, 0), 0)),
            pl.BlockSpec((None, tr, d), lambda bi, ti: (bi, ti, 0)),
            pl.BlockSpec((None, halo, d), lambda bi, ti: (bi, jnp.minimum((ti + 1) * hb, last_halo_block), 0)),
            pl.BlockSpec((w_rows, d), lambda bi, ti: (0, 0)),
            vec_spec, vec_spec, vec_spec,
        ],
        out_specs=pl.BlockSpec((None, tr, d), lambda bi, ti: (bi, ti, 0)),
        scratch_shapes=[pltpu.VMEM((tr + 2 * halo, d), F32), pltpu.VMEM((tr, d), F32)],
        compiler_params=_params(2, _vmem_limit(pipelined, resident)),
        name="conv_ln_swish",
    )(u, u, u, w_pad, vec(dw_b), vec(ln_g), vec(ln_b))


def _rope_tables(length):
    rows = length // GRID_W
    row = jnp.repeat(jnp.arange(rows, dtype=F32), GRID_W)
    col = jnp.tile(jnp.arange(GRID_W, dtype=F32), rows)
    n = DIFF_HEAD_DIM // 4
    inv = ROPE_BASE ** (-jnp.arange(n, dtype=F32) / n)
    ar, ac = row[:, None] * inv, col[:, None] * inv
    ang = jnp.concatenate([ar, ar, ac, ac], axis=-1)
    cos, sin = jnp.cos(ang), jnp.sin(ang)
    first_half = (jnp.arange(DIFF_HEAD_DIM) % (2 * n)) < n
    sin_lo = jnp.where(first_half, -sin, 0.0)
    sin_hi = jnp.where(first_half, 0.0, sin)
    return cos, sin_lo, sin_hi


def _mlp(h, w1, w2, x, gate, seq_len, *, tag):
    hid = _mm_call(_mm_relu2_kernel, h, w1, w1.shape[1], out_dtype=BF16,
                   bm=min(MM_BM, h.shape[0]), bn=min(MM_BN, w1.shape[1]), name=f"mlp_up_{tag}")
    return _mm_resid(hid, w2, x, gate, seq_len, bk=min(MLP_DOWN_BK, w2.shape[0]), name=f"mlp_down_{tag}")


def kernel(x, c, ctx, c_ctx, ada_w, ada_b, norm_mix_g, norm_mlp_g, norm_final_g, attn_w_qkv, attn_w_o, lambda_q1, lambda_k1, lambda_q2, lambda_k2, attn_subln_g, conv_pw1_w, conv_pw1_b, conv_dw_w, conv_dw_b, conv_ln_g, conv_ln_b, conv_pw2_w, conv_pw2_b, mlp_w1, mlp_w2):
    b, l, d = x.shape
    n_ctx = ctx.shape[1]
    depth = ada_w.shape[0]
    assert depth == 2 and attn_w_qkv.shape[0] == 1 and conv_pw1_w.shape[0] == 1
    n_heads = d // V_HEAD_DIM
    m = b * l

    rows = -(-(b + 1) // V7X_SUBLANES) * V7X_SUBLANES
    cvec = jnp.zeros((rows, d), F32).at[:b].set(c).at[b].set(c_ctx)
    mods = _adaln(cvec, ada_w, ada_b)

    def lat_mod(layer, j):
        return mods[layer, :b, j * d:(j + 1) * d].reshape(b, 1, d)

    def ctx_mod(layer, j):
        return mods[layer, b:b + 1, j * d:(j + 1) * d].reshape(1, 1, d)

    x_lat = x
    for layer in range(depth):
        kind, idx = layer % N_MIXERS, layer // N_MIXERS
        sh1, sc1, g1, sh2, sc2, g2 = (lat_mod(layer, j) for j in range(6))
        h_lat = _norm(x_lat, norm_mix_g[layer], sh1, sc1, out_dtype=BF16).reshape(m, d)
        x_flat = x_lat.reshape(m, d)
        if kind == 0:
            lam_init = 0.8 - 0.6 * math.exp(-0.3 * layer)
            h_ctx = _norm(ctx, norm_mix_g[layer], ctx_mod(layer, 0), ctx_mod(layer, 1), out_dtype=BF16)
            w_qkv = attn_w_qkv[idx].astype(BF16)
            cos, sin_lo, sin_hi = _rope_tables(l)
            bm, bn = min(MM_BM, l), min(MM_BN, d)
            tab_spec = pl.BlockSpec((bm, DIFF_HEAD_DIM), lambda i, j, k: (i % (l // bm), 0))
            qkv = _mm_call(
                functools.partial(_mm_rope_kernel, rope_cols=2 * d, bn=bn),
                h_lat, w_qkv, 3 * d,
                extras=[(cos, tab_spec), (sin_lo, tab_spec), (sin_hi, tab_spec)],
                out_dtype=BF16, bm=bm, bn=bn,
                extra_tile_bytes=3 * _nbytes((bm, DIFF_HEAD_DIM), F32), name="qkv_rope")
            kv_ctx = _mm_call(
                functools.partial(_mm_plain_kernel, nk=1), h_ctx.reshape(b * n_ctx, d), w_qkv, 2 * d,
                w_col_offsets=(d,), out_dtype=BF16, bm=min(MM_BM, b * n_ctx), bn=bn, name="kv_ctx")
            lam_vecs = jnp.stack([lambda_q1[idx], lambda_k1[idx], lambda_q2[idx], lambda_k2[idx]])
            o = _diff_attention(qkv.reshape(b, l, 3 * d), kv_ctx.reshape(b, n_ctx, 2 * d), lam_vecs,
                                attn_subln_g[idx], lam_init=lam_init, n_heads=n_heads)
            x_flat = _mm_resid(o.reshape(m, d), attn_w_o[idx].astype(BF16), x_flat, g1, l, name="attn_out")
        else:
            pw1 = conv_pw1_w[idx].astype(BF16)
            bm, bn = min(MM_BM, l), min(GLU_BN, d)
            bias = conv_pw1_b[idx].reshape(1, 2 * d)
            u = _mm_call(
                _mm_glu_kernel, h_lat, pw1, d, w_col_offsets=(0, d),
                extras=[(bias, pl.BlockSpec((1, bn), lambda i, j, k: (0, j))),
                        (bias, pl.BlockSpec((1, bn), lambda i, j, k: (0, j + d // bn)))],
                out_dtype=F32, bm=bm, bn=bn, n_temps=6, name="conv_pw1_glu")
            s = _conv_ln_swish(u.reshape(b, l, d), conv_dw_w[idx], conv_dw_b[idx],
                               conv_ln_g[idx], conv_ln_b[idx])
            x_flat = _mm_resid(s.reshape(m, d), conv_pw2_w[idx].astype(BF16), x_flat, g1, l,
                               bias=conv_pw2_b[idx], name="conv_pw2")
        h2 = _norm(x_flat.reshape(b, l, d), norm_mlp_g[layer], sh2, sc2, out_dtype=BF16).reshape(m, d)
        x_flat = _mlp(h2, mlp_w1[layer].astype(BF16), mlp_w2[layer].astype(BF16), x_flat, g2, l,
                      tag=str(layer))
        x_lat = x_flat.reshape(b, l, d)
    return _norm(x_lat, norm_final_g, out_dtype=F32)
```

```python
import functools
import math

import jax
import jax.numpy as jnp
from jax import lax
from jax.experimental import pallas as pl
from jax.experimental.pallas import tpu as pltpu

GRID_W = 64
DIFF_HEAD_DIM = 128
V_HEAD_DIM = 2 * DIFF_HEAD_DIM
ATTN_SCALE = DIFF_HEAD_DIM ** -0.5
ROPE_BASE = 10000.0
CONV_WIDTH = 31
CONV_PAD = CONV_WIDTH // 2
RMS_EPS = 1e-6
LN_EPS = 1e-5
N_MIXERS = 2

V7X_LANES = 128
V7X_SUBLANES = 8
V7X_VMEM_LIMIT_CAP = 60000 * 1024

MM_BM = 1024
MM_BN = 1024
MLP_DOWN_BK = 4096
GLU_BN = 512

F32 = jnp.float32
BF16 = jnp.bfloat16


def _nbytes(shape, dtype):
    return math.prod(shape) * jnp.dtype(dtype).itemsize


def _vmem_limit(pipelined_bytes, resident_bytes):
    return int(min(V7X_VMEM_LIMIT_CAP, 2 * pipelined_bytes + resident_bytes))


def _params(n_axes, vmem_bytes):
    return pltpu.CompilerParams(
        dimension_semantics=("arbitrary",) * n_axes, vmem_limit_bytes=vmem_bytes)


def _adaln_kernel(c_ref, w_ref, b_ref, o_ref):
    c = c_ref[...]
    s = (c * jax.nn.sigmoid(c)).astype(BF16)
    acc = jnp.dot(s, w_ref[...].astype(BF16), preferred_element_type=F32)
    o_ref[...] = acc + b_ref[...]


def _adaln(cvec, ada_w, ada_b, *, bn=512):
    depth, d, n = ada_w.shape
    rows = cvec.shape[0]
    pipelined = _nbytes((d, bn), F32) + _nbytes((rows, bn), F32) + _nbytes((1, bn), F32)
    resident = 2 * _nbytes((rows, d), F32) + _nbytes((d, bn), BF16)
    return pl.pallas_call(
        _adaln_kernel,
        out_shape=jax.ShapeDtypeStruct((depth, rows, n), F32),
        grid=(depth, n // bn),
        in_specs=[
            pl.BlockSpec((rows, d), lambda l, j: (0, 0)),
            pl.BlockSpec((None, d, bn), lambda l, j: (l, 0, j)),
            pl.BlockSpec((None, 1, bn), lambda l, j: (l, 0, j)),
        ],
        out_specs=pl.BlockSpec((None, rows, bn), lambda l, j: (l, 0, j)),
        compiler_params=_params(2, _vmem_limit(pipelined, resident)),
        name="adaln",
    )(cvec, ada_w, ada_b.reshape(depth, 1, n))


def _norm_kernel(x_ref, g_ref, *rest, modulate):
    o_ref = rest[-1]
    x = x_ref[...]
    y = x * lax.rsqrt(jnp.mean(x * x, axis=-1, keepdims=True) + RMS_EPS)
    y = y * g_ref[...]
    if modulate:
        sh_ref, sc_ref = rest[0], rest[1]
        y = y * (1.0 + sc_ref[...]) + sh_ref[...]
    o_ref[...] = y.astype(o_ref.dtype)


def _norm(x, g, shift=None, scale=None, *, out_dtype, tr=256):
    b, t, d = x.shape
    modulate = shift is not None
    operands = [x, g.reshape(1, d)]
    in_specs = [
        pl.BlockSpec((None, tr, d), lambda bi, ti: (bi, ti, 0)),
        pl.BlockSpec((1, d), lambda bi, ti: (0, 0)),
    ]
    if modulate:
        per_batch = shift.shape[0] == b
        vec_map = (lambda bi, ti: (bi, 0, 0)) if per_batch else (lambda bi, ti: (0, 0, 0))
        operands += [shift, scale]
        in_specs += [pl.BlockSpec((None, 1, d), vec_map)] * 2
    pipelined = _nbytes((tr, d), F32) + _nbytes((tr, d), out_dtype) + 3 * _nbytes((1, d), F32)
    resident = 3 * _nbytes((tr, d), F32)
    return pl.pallas_call(
        functools.partial(_norm_kernel, modulate=modulate),
        out_shape=jax.ShapeDtypeStruct((b, t, d), out_dtype),
        grid=(b, t // tr),
        in_specs=in_specs,
        out_specs=pl.BlockSpec((None, tr, d), lambda bi, ti: (bi, ti, 0)),
        compiler_params=_params(2, _vmem_limit(pipelined, resident)),
        name="rmsnorm_mod" if modulate else "rmsnorm",
    )(*operands)


def _dot(a_ref, w_ref):
    return jnp.dot(a_ref[...], w_ref[...], preferred_element_type=F32)


def _mm_plain_kernel(a_ref, w_ref, o_ref):
    o_ref[...] = _dot(a_ref, w_ref).astype(o_ref.dtype)


def _mm_rope_kernel(a_ref, w_ref, cos_ref, sin_lo_ref, sin_hi_ref, o_ref):
    acc = _dot(a_ref, w_ref)
    cos, sin_lo, sin_hi = cos_ref[...], sin_lo_ref[...], sin_hi_ref[...]
    quarter = DIFF_HEAD_DIM // 4
    for c0 in range(0, acc.shape[1], DIFF_HEAD_DIM):
        xh = acc[:, c0:c0 + DIFF_HEAD_DIM]
        r = (xh * cos
             + pltpu.roll(xh, DIFF_HEAD_DIM - quarter, 1) * sin_lo
             + pltpu.roll(xh, quarter, 1) * sin_hi)
        o_ref[:, c0:c0 + DIFF_HEAD_DIM] = r.astype(o_ref.dtype)


def _mm_relu2_kernel(a_ref, w_ref, o_ref):
    acc = _dot(a_ref, w_ref)
    r = jnp.maximum(acc, 0.0)
    o_ref[...] = (r * r).astype(o_ref.dtype)


def _mm_glu_kernel(a_ref, wa_ref, wg_ref, ba_ref, bg_ref, o_ref):
    lin = _dot(a_ref, wa_ref) + ba_ref[...]
    gate = _dot(a_ref, wg_ref) + bg_ref[...]
    o_ref[...] = (lin * jax.nn.sigmoid(gate)).astype(o_ref.dtype)


def _mm_resid_kernel(a_ref, w_ref, x_ref, gate_ref, *rest, nk, has_bias):
    bias_ref, o_ref = (rest[0], rest[1]) if has_bias else (None, rest[0])
    if nk == 1:
        y = _dot(a_ref, w_ref)
        if has_bias:
            y = y + bias_ref[...]
        o_ref[...] = x_ref[...] + gate_ref[...] * y
        return

    @pl.when(pl.program_id(2) == 0)
    def _():
        x0 = x_ref[...]
        o_ref[...] = x0 + gate_ref[...] * bias_ref[...] if has_bias else x0

    o_ref[...] += gate_ref[...] * _dot(a_ref, w_ref)


def _mm_call(body, a, w, layer, n_cols, *, w_col_offsets=(0,), extras=(), out_dtype, bm, bn, bk=None,
             extra_tile_bytes=0, n_temps=3, name):
    m, kdim = a.shape
    bk = kdim if bk is None else bk
    nk = kdim // bk
    assert m % bm == 0 and n_cols % bn == 0 and kdim % bk == 0, (name, a.shape, n_cols, bm, bn, bk)
    in_specs = [pl.BlockSpec((bm, bk), lambda i, j, k: (i, k))]
    for off in w_col_offsets:
        assert off % bn == 0
        in_specs.append(pl.BlockSpec((None, bk, bn), lambda i, j, k, ob=off // bn: (layer, k, j + ob)))
    in_specs += [spec for _, spec in extras]
    pipelined = (_nbytes((bm, bk), BF16) + len(w_col_offsets) * _nbytes((bk, bn), BF16)
                 + _nbytes((bm, bn), out_dtype) + extra_tile_bytes)
    resident = n_temps * _nbytes((bm, bn), F32)
    return pl.pallas_call(
        body,
        out_shape=jax.ShapeDtypeStruct((m, n_cols), out_dtype),
        grid=(m // bm, n_cols // bn, nk),
        in_specs=in_specs,
        out_specs=pl.BlockSpec((bm, bn), lambda i, j, k: (i, j)),
        compiler_params=_params(3, _vmem_limit(pipelined, resident)),
        name=name,
    )(a, *([w] * len(w_col_offsets)), *[arr for arr, _ in extras])


def _mm_resid(a, w, layer, x, gate, seq_len, *, bias=None, bk=None, name):
    n = w.shape[2]
    bm, bn = min(MM_BM, seq_len), min(MM_BN, n)
    bk = a.shape[1] if bk is None else bk
    nk = a.shape[1] // bk
    blocks_per_batch = seq_len // bm
    extras = [
        (x, pl.BlockSpec((bm, bn), lambda i, j, k: (i, j))),
        (gate, pl.BlockSpec((None, 1, bn), lambda i, j, k: (i // blocks_per_batch, 0, j))),
    ]
    if bias is not None:
        extras.append((bias.reshape(1, n), pl.BlockSpec((1, bn), lambda i, j, k: (0, j))))
    return _mm_call(
        functools.partial(_mm_resid_kernel, nk=nk, has_bias=bias is not None),
        a, w, layer, n, extras=extras, out_dtype=F32, bm=bm, bn=bn, bk=bk,
        extra_tile_bytes=_nbytes((bm, bn), F32), n_temps=2, name=name)


def _lane_fold_sum(x):
    out = x[:, 0:V7X_LANES]
    for c0 in range(V7X_LANES, x.shape[1], V7X_LANES):
        out = out + x[:, c0:c0 + V7X_LANES]
    return out


def _attn_kernel(lam_vecs_ref, subln_g_ref, q_ref, kc_ref, k_ref, vc_ref, v_ref, o_ref,
                 lam_ref, *, lam_init, key_chunk, q_sub):
    dk = DIFF_HEAD_DIM
    n_ctx, n_lat = kc_ref.shape[0], k_ref.shape[0]

    first = jnp.logical_and(pl.program_id(0) == 0,
                            jnp.logical_and(pl.program_id(1) == 0, pl.program_id(2) == 0))

    @pl.when(first)
    def _():
        lv = lam_vecs_ref[...]
        d1 = jnp.sum(lv[0:1, :] * lv[1:2, :], axis=-1, keepdims=True)
        d2 = jnp.sum(lv[2:3, :] * lv[3:4, :], axis=-1, keepdims=True)
        lam_ref[...] = jnp.broadcast_to(jnp.exp(d1) - jnp.exp(d2) + lam_init, lam_ref.shape)

    groups = [(kc_ref, vc_ref, 0, n_ctx)]
    groups += [(k_ref, v_ref, r0, key_chunk) for r0 in range(0, n_lat, key_chunk)]

    lam = lam_ref[0:1, 0:1]
    for q0 in range(0, q_ref.shape[0], q_sub):
        heads = []
        for comp in range(2):
            qc = q_ref[q0:q0 + q_sub, comp * dk:(comp + 1) * dk]
            m = lsum = acc = None
            for key_ref, val_ref, r0, rows in groups:
                kk = key_ref[r0:r0 + rows, comp * dk:(comp + 1) * dk]
                s = lax.dot_general(qc, kk, (((1,), (1,)), ((), ())), preferred_element_type=F32)
                gmax = jnp.max(s, axis=-1, keepdims=True)
                m_new = gmax if m is None else jnp.maximum(m, gmax)
                e = jnp.exp2(s - m_new)
                esum = _lane_fold_sum(e)
                pv = jnp.dot(e.astype(BF16), val_ref[r0:r0 + rows, :], preferred_element_type=F32)
                if m is None:
                    lsum, acc = esum, pv
                else:
                    alpha = jnp.exp2(m - m_new)
                    lsum, acc = lsum * alpha + esum, acc * alpha + pv
                m = m_new
            heads.append(acc / jnp.sum(lsum, axis=-1, keepdims=True))

        o = heads[0] - lam * heads[1]
        o = o * lax.rsqrt(jnp.mean(o * o, axis=-1, keepdims=True) + RMS_EPS)
        o = (o * subln_g_ref[...]) * (1.0 - lam_init)
        o_ref[q0:q0 + q_sub, :] = o.astype(o_ref.dtype)


def _diff_attention(qkv, kv_ctx, lam_vecs, subln_g, *, lam_init, n_heads, bq=2048, q_sub=512, key_chunk=1024):
    b, l, d3 = qkv.shape
    d = d3 // 3
    c = kv_ctx.shape[1]
    bq, key_chunk = min(bq, l), min(key_chunk, l)
    q_sub = min(q_sub, bq)
    hw = V_HEAD_DIM
    assert hw == 2 * DIFF_HEAD_DIM and n_heads * hw == d and l % key_chunk == 0
    hb = d // hw
    pipelined = (2 * _nbytes((bq, hw), BF16) + 2 * _nbytes((c, hw), BF16) + 2 * _nbytes((l, hw), BF16))
    resident = (bq // q_sub) * (8 * _nbytes((q_sub, key_chunk), F32) + 8 * _nbytes((q_sub, hw), F32))
    return pl.pallas_call(
        functools.partial(_attn_kernel, lam_init=lam_init, key_chunk=key_chunk, q_sub=q_sub),
        out_shape=jax.ShapeDtypeStruct((b, l, d), BF16),
        grid=(b, n_heads, l // bq),
        in_specs=[
            pl.BlockSpec((4, DIFF_HEAD_DIM), lambda bi, h, qi: (0, 0)),
            pl.BlockSpec((1, hw), lambda bi, h, qi: (0, 0)),
            pl.BlockSpec((None, bq, hw), lambda bi, h, qi: (bi, qi, h)),
            pl.BlockSpec((None, c, hw), lambda bi, h, qi: (bi, 0, h)),
            pl.BlockSpec((None, l, hw), lambda bi, h, qi: (bi, 0, hb + h)),
            pl.BlockSpec((None, c, hw), lambda bi, h, qi: (bi, 0, hb + h)),
            pl.BlockSpec((None, l, hw), lambda bi, h, qi: (bi, 0, 2 * hb + h)),
        ],
        out_specs=pl.BlockSpec((None, bq, hw), lambda bi, h, qi: (bi, qi, h)),
        scratch_shapes=[pltpu.VMEM((V7X_SUBLANES, V7X_LANES), F32)],
        compiler_params=_params(3, _vmem_limit(pipelined, resident)),
        name="diff_attention",
    )(lam_vecs, subln_g.reshape(1, hw), qkv, kv_ctx, qkv, kv_ctx, qkv)


def _conv_ln_kernel(prev_ref, u_ref, next_ref, w_ref, dwb_ref, lng_ref, lnb_ref, o_ref,
                    ubuf, rbuf, ybuf, *, halo, rb, cb):
    tr, d = u_ref.shape
    rows = tr + 2 * halo
    ti, nt = pl.program_id(1), pl.num_programs(1)
    ubuf[0:halo, :] = jnp.where(ti > 0, prev_ref[...], 0.0)
    ubuf[halo:halo + tr, :] = u_ref[...]
    ubuf[halo + tr:, :] = jnp.where(ti < nt - 1, next_ref[...], 0.0)

    def conv_chunk(ci, carry):
        c0 = pl.multiple_of(ci * cb, cb)
        u_chunk = ubuf[:, pl.ds(c0, cb)]
        for s in range(1, V7X_SUBLANES):
            rbuf[s - 1] = pltpu.roll(u_chunk, rows - s, 0)
        for r0 in range(0, tr, rb):
            acc = jnp.zeros((rb, cb), F32)
            for tap in range(CONV_WIDTH):
                row = halo + r0 + tap - CONV_PAD
                s, base = row % V7X_SUBLANES, row - row % V7X_SUBLANES
                if s == 0:
                    src = ubuf[base:base + rb, pl.ds(c0, cb)]
                else:
                    src = rbuf[s - 1, base:base + rb, :]
                acc = acc + src * w_ref[tap:tap + 1, pl.ds(c0, cb)]
            ybuf[r0:r0 + rb, pl.ds(c0, cb)] = acc + dwb_ref[:, pl.ds(c0, cb)]
        return carry

    lax.fori_loop(0, d // cb, conv_chunk, 0)

    for r0 in range(0, tr, rb):
        y = ybuf[r0:r0 + rb, :]
        mu = jnp.mean(y, axis=-1, keepdims=True)
        yc = y - mu
        var = jnp.mean(yc * yc, axis=-1, keepdims=True)
        z = yc * lax.rsqrt(var + LN_EPS) * lng_ref[...] + lnb_ref[...]
        o_ref[r0:r0 + rb, :] = (z * jax.nn.sigmoid(z)).astype(o_ref.dtype)


def _conv_ln_swish(u, dw_w, dw_b, ln_g, ln_b, *, tr=256, halo=16, rb=64, cb=512):
    b, l, d = u.shape
    assert halo >= CONV_PAD and halo % V7X_SUBLANES == 0 and tr % halo == 0 and l % tr == 0
    assert halo - CONV_PAD >= 0 and CONV_WIDTH - 1 - CONV_PAD <= halo
    hb = tr // halo
    last_halo_block = l // halo - 1
    w_rows = -(-CONV_WIDTH // V7X_SUBLANES) * V7X_SUBLANES
    w_pad = jnp.zeros((w_rows, d), F32).at[:CONV_WIDTH].set(dw_w)
    vec = lambda v: v.reshape(1, d)
    vec_spec = pl.BlockSpec((1, d), lambda bi, ti: (0, 0))
    rows = tr + 2 * halo
    pipelined = (_nbytes((tr, d), F32) + 2 * _nbytes((halo, d), F32) + _nbytes((tr, d), BF16)
                 + _nbytes((w_rows + 3, d), F32))
    resident = (_nbytes((rows, d), F32) + _nbytes((V7X_SUBLANES - 1, rows, cb), F32) + _nbytes((tr, d), F32)
                + 2 * _nbytes((rows, cb), F32) + 8 * _nbytes((rb, d), F32))
    return pl.pallas_call(
        functools.partial(_conv_ln_kernel, halo=halo, rb=rb, cb=cb),
        out_shape=jax.ShapeDtypeStruct((b, l, d), BF16),
        grid=(b, l // tr),
        in_specs=[
            pl.BlockSpec((None, halo, d), lambda bi, ti: (bi, jnp.maximum(ti * hb - 1, 0), 0)),
            pl.BlockSpec((None, tr, d), lambda bi, ti: (bi, ti, 0)),
            pl.BlockSpec((None, halo, d), lambda bi, ti: (bi, jnp.minimum((ti + 1) * hb, last_halo_block), 0)),
            pl.BlockSpec((w_rows, d), lambda bi, ti: (0, 0)),
            vec_spec, vec_spec, vec_spec,
        ],
        out_specs=pl.BlockSpec((None, tr, d), lambda bi, ti: (bi, ti, 0)),
        scratch_shapes=[pltpu.VMEM((rows, d), F32),
                        pltpu.VMEM((V7X_SUBLANES - 1, rows, cb), F32),
                        pltpu.VMEM((tr, d), F32)],
        compiler_params=_params(2, _vmem_limit(pipelined, resident)),
        name="conv_ln_swish",
    )(u, u, u, w_pad, vec(dw_b), vec(ln_g), vec(ln_b))


def _rope_tables(length):
    rows = length // GRID_W
    row = jnp.repeat(jnp.arange(rows, dtype=F32), GRID_W)
    col = jnp.tile(jnp.arange(GRID_W, dtype=F32), rows)
    n = DIFF_HEAD_DIM // 4
    inv = ROPE_BASE ** (-jnp.arange(n, dtype=F32) / n)
    ar, ac = row[:, None] * inv, col[:, None] * inv
    ang = jnp.concatenate([ar, ar, ac, ac], axis=-1)
    cos, sin = jnp.cos(ang), jnp.sin(ang)
    first_half = (jnp.arange(DIFF_HEAD_DIM) % (2 * n)) < n
    sin_lo = jnp.where(first_half, -sin, 0.0)
    sin_hi = jnp.where(first_half, 0.0, sin)
    q_scale = ATTN_SCALE * math.log2(math.e)
    one, zero = jnp.ones_like(cos), jnp.zeros_like(cos)
    return (jnp.stack([cos * q_scale, cos, one]),
            jnp.stack([sin_lo * q_scale, sin_lo, zero]),
            jnp.stack([sin_hi * q_scale, sin_hi, zero]))


def _mlp(h, w1, w2, layer, x, gate, seq_len):
    n_ff = w1.shape[2]
    hid = _mm_call(_mm_relu2_kernel, h, w1, layer, n_ff, out_dtype=BF16,
                   bm=min(MM_BM, h.shape[0]), bn=min(MM_BN, n_ff), name=f"mlp_up_{layer}")
    return _mm_resid(hid, w2, layer, x, gate, seq_len, bk=min(MLP_DOWN_BK, n_ff), name=f"mlp_down_{layer}")


def kernel(x, c, ctx, c_ctx, ada_w, ada_b, norm_mix_g, norm_mlp_g, norm_final_g, attn_w_qkv, attn_w_o, lambda_q1, lambda_k1, lambda_q2, lambda_k2, attn_subln_g, conv_pw1_w, conv_pw1_b, conv_dw_w, conv_dw_b, conv_ln_g, conv_ln_b, conv_pw2_w, conv_pw2_b, mlp_w1, mlp_w2):
    b, l, d = x.shape
    n_ctx = ctx.shape[1]
    depth = ada_w.shape[0]
    assert depth == 2 and attn_w_qkv.shape[0] == 1 and conv_pw1_w.shape[0] == 1
    n_heads = d // V_HEAD_DIM
    m = b * l

    rows = -(-(b + 1) // V7X_SUBLANES) * V7X_SUBLANES
    cvec = jnp.zeros((rows, d), F32).at[:b].set(c).at[b].set(c_ctx)
    mods = _adaln(cvec, ada_w, ada_b)

    def lat_mod(layer, j):
        return mods[layer, :b, j * d:(j + 1) * d].reshape(b, 1, d)

    def ctx_mod(layer, j):
        return mods[layer, b:b + 1, j * d:(j + 1) * d].reshape(1, 1, d)

    w_qkv, w_o = attn_w_qkv.astype(BF16), attn_w_o.astype(BF16)
    w_pw1, w_pw2 = conv_pw1_w.astype(BF16), conv_pw2_w.astype(BF16)
    w_up, w_down = mlp_w1.astype(BF16), mlp_w2.astype(BF16)

    x_lat = x
    for layer in range(depth):
        kind, idx = layer % N_MIXERS, layer // N_MIXERS
        sh1, sc1, g1, sh2, sc2, g2 = (lat_mod(layer, j) for j in range(6))
        h_lat = _norm(x_lat, norm_mix_g[layer], sh1, sc1, out_dtype=BF16).reshape(m, d)
        x_flat = x_lat.reshape(m, d)
        bm = min(MM_BM, l)
        if kind == 0:
            lam_init = 0.8 - 0.6 * math.exp(-0.3 * layer)
            h_ctx = _norm(ctx, norm_mix_g[layer], ctx_mod(layer, 0), ctx_mod(layer, 1), out_dtype=BF16)
            bn = min(MM_BN, d)
            tab_spec = pl.BlockSpec((None, bm, DIFF_HEAD_DIM),
                                    lambda i, j, k: (j // (d // bn), i % (l // bm), 0))
            qkv = _mm_call(
                _mm_rope_kernel, h_lat, w_qkv, idx, 3 * d,
                extras=[(tab, tab_spec) for tab in _rope_tables(l)],
                out_dtype=BF16, bm=bm, bn=bn,
                extra_tile_bytes=3 * _nbytes((bm, DIFF_HEAD_DIM), F32), name="qkv_rope")
            kv_ctx = _mm_call(
                _mm_plain_kernel, h_ctx.reshape(b * n_ctx, d), w_qkv, idx, 2 * d,
                w_col_offsets=(d,), out_dtype=BF16, bm=min(MM_BM, b * n_ctx), bn=bn, name="kv_ctx")
            lam_vecs = jnp.stack([lambda_q1[idx], lambda_k1[idx], lambda_q2[idx], lambda_k2[idx]])
            o = _diff_attention(qkv.reshape(b, l, 3 * d), kv_ctx.reshape(b, n_ctx, 2 * d), lam_vecs,
                                attn_subln_g[idx], lam_init=lam_init, n_heads=n_heads)
            x_flat = _mm_resid(o.reshape(m, d), w_o, idx, x_flat, g1, l, name="attn_out")
        else:
            bn = min(GLU_BN, d)
            bias = conv_pw1_b[idx].reshape(1, 2 * d)
            u = _mm_call(
                _mm_glu_kernel, h_lat, w_pw1, idx, d, w_col_offsets=(0, d),
                extras=[(bias, pl.BlockSpec((1, bn), lambda i, j, k: (0, j))),
                        (bias, pl.BlockSpec((1, bn), lambda i, j, k: (0, j + d // bn)))],
                out_dtype=F32, bm=bm, bn=bn, n_temps=6, name="conv_pw1_glu")
            s = _conv_ln_swish(u.reshape(b, l, d), conv_dw_w[idx], conv_dw_b[idx],
                               conv_ln_g[idx], conv_ln_b[idx])
            x_flat = _mm_resid(s.reshape(m, d), w_pw2, idx, x_flat, g1, l,
                               bias=conv_pw2_b[idx], name="conv_pw2")
        h2 = _norm(x_flat.reshape(b, l, d), norm_mlp_g[layer], sh2, sc2, out_dtype=BF16).reshape(m, d)
        x_flat = _mlp(h2, w_up, w_down, layer, x_flat, g2, l)
        x_lat = x_flat.reshape(b, l, d)
    return _norm(x_lat, norm_final_g, out_dtype=F32)
```

```python
import functools
import math

import jax
import jax.numpy as jnp
from jax import lax
from jax.experimental import pallas as pl
from jax.experimental.pallas import tpu as pltpu

GRID_W = 64
DIFF_HEAD_DIM = 128
V_HEAD_DIM = 2 * DIFF_HEAD_DIM
ATTN_SCALE = DIFF_HEAD_DIM ** -0.5
ROPE_BASE = 10000.0
CONV_WIDTH = 31
CONV_PAD = CONV_WIDTH // 2
RMS_EPS = 1e-6
LN_EPS = 1e-5
N_MIXERS = 2

V7X_LANES = 128
V7X_SUBLANES = 8
V7X_VMEM_LIMIT_CAP = 60000 * 1024

MM_BM = 1024
MM_BN = 1024
MLP_DOWN_BK = 4096
GLU_BN = 512

F32 = jnp.float32
BF16 = jnp.bfloat16


def _nbytes(shape, dtype):
    return math.prod(shape) * jnp.dtype(dtype).itemsize


def _vmem_limit(pipelined_bytes, resident_bytes):
    return int(min(V7X_VMEM_LIMIT_CAP, 2 * pipelined_bytes + resident_bytes))


def _params(n_axes, vmem_bytes):
    return pltpu.CompilerParams(
        dimension_semantics=("arbitrary",) * n_axes, vmem_limit_bytes=vmem_bytes)


def _adaln_kernel(c_ref, w_ref, b_ref, o_ref):
    c = c_ref[...]
    s = (c * jax.nn.sigmoid(c)).astype(BF16)
    acc = jnp.dot(s, w_ref[...].astype(BF16), preferred_element_type=F32)
    o_ref[...] = acc + b_ref[...]


def _adaln(cvec, ada_w, ada_b, *, bn=512):
    depth, d, n = ada_w.shape
    rows = cvec.shape[0]
    pipelined = _nbytes((d, bn), F32) + _nbytes((rows, bn), F32) + _nbytes((1, bn), F32)
    resident = 2 * _nbytes((rows, d), F32) + _nbytes((d, bn), BF16)
    return pl.pallas_call(
        _adaln_kernel,
        out_shape=jax.ShapeDtypeStruct((depth, rows, n), F32),
        grid=(depth, n // bn),
        in_specs=[
            pl.BlockSpec((rows, d), lambda l, j: (0, 0)),
            pl.BlockSpec((None, d, bn), lambda l, j: (l, 0, j)),
            pl.BlockSpec((None, 1, bn), lambda l, j: (l, 0, j)),
        ],
        out_specs=pl.BlockSpec((None, rows, bn), lambda l, j: (l, 0, j)),
        compiler_params=_params(2, _vmem_limit(pipelined, resident)),
        name="adaln",
    )(cvec, ada_w, ada_b.reshape(depth, 1, n))


def _norm_kernel(x_ref, g_ref, *rest, modulate, rb):
    o_ref = rest[-1]
    x = x_ref[...]
    rstd = lax.rsqrt(jnp.mean(x * x, axis=-1, keepdims=True) + RMS_EPS)
    gain = g_ref[...]
    if modulate:
        sh_ref, sc_ref = rest[0], rest[1]
        gain = gain * (1.0 + sc_ref[...])
    for r0 in range(0, x_ref.shape[0], rb):
        y = (x_ref[r0:r0 + rb, :] * rstd[r0:r0 + rb]) * gain
        if modulate:
            y = y + sh_ref[...]
        o_ref[r0:r0 + rb, :] = y.astype(o_ref.dtype)


def _norm(x, g, shift=None, scale=None, *, out_dtype, tr=256):
    b, t, d = x.shape
    modulate = shift is not None
    operands = [x, g.reshape(1, d)]
    in_specs = [
        pl.BlockSpec((None, tr, d), lambda bi, ti: (bi, ti, 0)),
        pl.BlockSpec((1, d), lambda bi, ti: (0, 0)),
    ]
    if modulate:
        per_batch = shift.shape[0] == b
        vec_map = (lambda bi, ti: (bi, 0, 0)) if per_batch else (lambda bi, ti: (0, 0, 0))
        operands += [shift, scale]
        in_specs += [pl.BlockSpec((None, 1, d), vec_map)] * 2
    pipelined = _nbytes((tr, d), F32) + _nbytes((tr, d), out_dtype) + 3 * _nbytes((1, d), F32)
    resident = 3 * _nbytes((tr, d), F32)
    return pl.pallas_call(
        functools.partial(_norm_kernel, modulate=modulate, rb=min(32, tr)),
        out_shape=jax.ShapeDtypeStruct((b, t, d), out_dtype),
        grid=(b, t // tr),
        in_specs=in_specs,
        out_specs=pl.BlockSpec((None, tr, d), lambda bi, ti: (bi, ti, 0)),
        compiler_params=_params(2, _vmem_limit(pipelined, resident)),
        name="rmsnorm_mod" if modulate else "rmsnorm",
    )(*operands)


def _cast_plan(jobs, grid, step_of):
    n_steps = math.prod(grid)
    in_specs, out_specs, out_shapes, tile_bytes = [], [], [], 0
    for src, layer in jobs:
        _, kdim, n = src.shape
        rows = kdim // n_steps
        assert rows * n_steps == kdim and rows % (2 * V7X_SUBLANES) == 0, (src.shape, n_steps)
        in_specs.append(pl.BlockSpec((None, rows, n), lambda *ids, layer=layer: (layer, step_of(*ids), 0)))
        out_specs.append(pl.BlockSpec((rows, n), lambda *ids: (step_of(*ids), 0)))
        out_shapes.append(jax.ShapeDtypeStruct((kdim, n), BF16))
        tile_bytes += _nbytes((rows, n), F32) + _nbytes((rows, n), BF16)
    return in_specs, out_specs, out_shapes, tile_bytes


def _hosting_casts(body, n_in, n_out, n_casts):
    if n_casts == 0:
        return body

    def wrapped(*refs):
        ins, rest = refs[:n_in], refs[n_in:]
        cast_ins, rest = rest[:n_casts], rest[n_casts:]
        outs, rest = rest[:n_out], rest[n_out:]
        cast_outs, scratch = rest[:n_casts], rest[n_casts:]
        body(*ins, *outs, *scratch)
        for src_ref, dst_ref in zip(cast_ins, cast_outs):
            dst_ref[...] = src_ref[...].astype(dst_ref.dtype)

    return wrapped


def _dot(a_ref, w_ref):
    return jnp.dot(a_ref[...], w_ref[...], preferred_element_type=F32)


def _mm_plain_kernel(a_ref, w_ref, o_ref):
    o_ref[...] = _dot(a_ref, w_ref).astype(o_ref.dtype)


def _qkv_kernel(a_ref, wqk_ref, wv_ref, cos_ref, sin_lo_ref, sin_hi_ref, qk_ref, v_ref):
    a = a_ref[...]
    acc = jnp.dot(a, wqk_ref[...], preferred_element_type=F32)
    cos, sin_lo, sin_hi = cos_ref[...], sin_lo_ref[...], sin_hi_ref[...]
    quarter = DIFF_HEAD_DIM // 4
    for c0 in range(0, acc.shape[1], DIFF_HEAD_DIM):
        xh = acc[:, c0:c0 + DIFF_HEAD_DIM]
        r = (xh * cos
             + pltpu.roll(xh, DIFF_HEAD_DIM - quarter, 1) * sin_lo
             + pltpu.roll(xh, quarter, 1) * sin_hi)
        qk_ref[:, c0:c0 + DIFF_HEAD_DIM] = r.astype(qk_ref.dtype)
    v_ref[...] = jnp.dot(a, wv_ref[...], preferred_element_type=F32).astype(v_ref.dtype)


def _qkv_proj(h, w_qkv, layer, seq_len, tables, *, bm, bn_qk):
    m, d = h.shape
    n_tiles = 2 * d // bn_qk
    bn_v = d // n_tiles
    assert m % bm == 0 and seq_len % bm == 0 and d % bn_qk == 0 and bn_v % V7X_LANES == 0
    tab_spec = pl.BlockSpec((None, bm, DIFF_HEAD_DIM),
                            lambda i, j: (j // (d // bn_qk), i % (seq_len // bm), 0))
    pipelined = (_nbytes((bm, d), BF16) + _nbytes((d, bn_qk + bn_v), BF16) + _nbytes((bm, bn_qk + bn_v), BF16)
                 + 3 * _nbytes((bm, DIFF_HEAD_DIM), F32))
    resident = 3 * _nbytes((bm, bn_qk), F32)
    return pl.pallas_call(
        _qkv_kernel,
        out_shape=(jax.ShapeDtypeStruct((m, 2 * d), BF16), jax.ShapeDtypeStruct((m, d), BF16)),
        grid=(m // bm, n_tiles),
        in_specs=[
            pl.BlockSpec((bm, d), lambda i, j: (i, 0)),
            pl.BlockSpec((None, d, bn_qk), lambda i, j: (layer, 0, j)),
            pl.BlockSpec((None, d, bn_v), lambda i, j: (layer, 0, 2 * d // bn_v + j)),
            tab_spec, tab_spec, tab_spec,
        ],
        out_specs=(pl.BlockSpec((bm, bn_qk), lambda i, j: (i, j)),
                   pl.BlockSpec((bm, bn_v), lambda i, j: (i, j))),
        compiler_params=_params(2, _vmem_limit(pipelined, resident)),
        name="qkv_rope",
    )(h, w_qkv, w_qkv, *tables)


def _mm_relu2_kernel(a_ref, w_ref, o_ref):
    acc = _dot(a_ref, w_ref)
    r = jnp.maximum(acc, 0.0)
    o_ref[...] = (r * r).astype(o_ref.dtype)


def _mm_glu_kernel(a_ref, wa_ref, wg_ref, ba_ref, bg_ref, o_ref):
    lin = _dot(a_ref, wa_ref) + ba_ref[...]
    gate = _dot(a_ref, wg_ref) + bg_ref[...]
    o_ref[...] = (lin * jax.nn.sigmoid(gate)).astype(o_ref.dtype)


def _mm_resid_kernel(a_ref, w_ref, x_ref, gate_ref, *rest, nk, has_bias):
    bias_ref, o_ref = (rest[0], rest[1]) if has_bias else (None, rest[0])
    if nk == 1:
        y = _dot(a_ref, w_ref)
        if has_bias:
            y = y + bias_ref[...]
        o_ref[...] = x_ref[...] + gate_ref[...] * y
        return

    @pl.when(pl.program_id(2) == 0)
    def _():
        x0 = x_ref[...]
        o_ref[...] = x0 + gate_ref[...] * bias_ref[...] if has_bias else x0

    o_ref[...] += gate_ref[...] * _dot(a_ref, w_ref)


def _mm_call(body, a, w, layer, n_cols, *, w_col_offsets=(0,), extras=(), out_dtype, bm, bn, bk=None,
             extra_tile_bytes=0, n_temps=3, casts=(), name):
    m, kdim = a.shape
    bk = kdim if bk is None else bk
    nk = kdim // bk
    assert m % bm == 0 and n_cols % bn == 0 and kdim % bk == 0, (name, a.shape, n_cols, bm, bn, bk)
    grid = (m // bm, n_cols // bn, nk)
    in_specs = [pl.BlockSpec((bm, bk), lambda i, j, k: (i, k))]
    for off in w_col_offsets:
        assert off % bn == 0
        in_specs.append(pl.BlockSpec((None, bk, bn), lambda i, j, k, ob=off // bn: (layer, k, j + ob)))
    in_specs += [spec for _, spec in extras]
    cast_in, cast_out, cast_shapes, cast_bytes = _cast_plan(
        casts, grid, lambda i, j, k: (i * grid[1] + j) * grid[2] + k)
    pipelined = (_nbytes((bm, bk), BF16) + len(w_col_offsets) * _nbytes((bk, bn), BF16)
                 + _nbytes((bm, bn), out_dtype) + extra_tile_bytes + cast_bytes)
    resident = n_temps * _nbytes((bm, bn), F32)
    outs = pl.pallas_call(
        _hosting_casts(body, len(in_specs), 1, len(casts)),
        out_shape=[jax.ShapeDtypeStruct((m, n_cols), out_dtype)] + cast_shapes,
        grid=grid,
        in_specs=in_specs + cast_in,
        out_specs=[pl.BlockSpec((bm, bn), lambda i, j, k: (i, j))] + cast_out,
        compiler_params=_params(3, _vmem_limit(pipelined, resident)),
        name=name,
    )(a, *([w] * len(w_col_offsets)), *[arr for arr, _ in extras], *[src for src, _ in casts])
    return outs[0] if not casts else tuple(outs)


def _mm_resid(a, w, layer, x, gate, seq_len, *, bias=None, bk=None, name):
    n = w.shape[2]
    bm, bn = min(MM_BM, seq_len), min(MM_BN, n)
    bk = a.shape[1] if bk is None else bk
    nk = a.shape[1] // bk
    blocks_per_batch = seq_len // bm
    extras = [
        (x, pl.BlockSpec((bm, bn), lambda i, j, k: (i, j))),
        (gate, pl.BlockSpec((None, 1, bn), lambda i, j, k: (i // blocks_per_batch, 0, j))),
    ]
    if bias is not None:
        extras.append((bias.reshape(1, n), pl.BlockSpec((1, bn), lambda i, j, k: (0, j))))
    return _mm_call(
        functools.partial(_mm_resid_kernel, nk=nk, has_bias=bias is not None),
        a, w, layer, n, extras=extras, out_dtype=F32, bm=bm, bn=bn, bk=bk,
        extra_tile_bytes=_nbytes((bm, bn), F32), n_temps=2, name=name)


def _lane_fold_sum(x):
    out = x[:, 0:V7X_LANES]
    for c0 in range(V7X_LANES, x.shape[1], V7X_LANES):
        out = out + x[:, c0:c0 + V7X_LANES]
    return out


def _attn_kernel(lam_vecs_ref, subln_g_ref, q_ref, kc_ref, k_ref, vc_ref, v_ref, o_ref,
                 lam_ref, *, lam_init, key_chunk, q_sub):
    dk = DIFF_HEAD_DIM
    n_ctx, n_lat = kc_ref.shape[0], k_ref.shape[0]

    first = jnp.logical_and(pl.program_id(0) == 0,
                            jnp.logical_and(pl.program_id(1) == 0, pl.program_id(2) == 0))

    @pl.when(first)
    def _():
        lv = lam_vecs_ref[...]
        d1 = jnp.sum(lv[0:1, :] * lv[1:2, :], axis=-1, keepdims=True)
        d2 = jnp.sum(lv[2:3, :] * lv[3:4, :], axis=-1, keepdims=True)
        lam_ref[...] = jnp.broadcast_to(jnp.exp(d1) - jnp.exp(d2) + lam_init, lam_ref.shape)

    groups = [(kc_ref, vc_ref, 0, n_ctx)]
    groups += [(k_ref, v_ref, r0, key_chunk) for r0 in range(0, n_lat, key_chunk)]

    lam = lam_ref[0:1, 0:1]
    for q0 in range(0, q_ref.shape[0], q_sub):
        heads = []
        for comp in range(2):
            qc = q_ref[q0:q0 + q_sub, comp * dk:(comp + 1) * dk]
            m = lsum = acc = None
            for key_ref, val_ref, r0, rows in groups:
                kk = key_ref[r0:r0 + rows, comp * dk:(comp + 1) * dk]
                s = lax.dot_general(qc, kk, (((1,), (1,)), ((), ())), preferred_element_type=F32)
                gmax = jnp.max(s, axis=-1, keepdims=True)
                m_new = gmax if m is None else jnp.maximum(m, gmax)
                e = jnp.exp2(s - m_new)
                esum = _lane_fold_sum(e)
                pv = jnp.dot(e.astype(BF16), val_ref[r0:r0 + rows, :], preferred_element_type=F32)
                if m is None:
                    lsum, acc = esum, pv
                else:
                    alpha = jnp.exp2(m - m_new)
                    lsum, acc = lsum * alpha + esum, acc * alpha + pv
                m = m_new
            heads.append(acc / jnp.sum(lsum, axis=-1, keepdims=True))

        o = heads[0] - lam * heads[1]
        o = o * lax.rsqrt(jnp.mean(o * o, axis=-1, keepdims=True) + RMS_EPS)
        o = (o * subln_g_ref[...]) * (1.0 - lam_init)
        o_ref[q0:q0 + q_sub, :] = o.astype(o_ref.dtype)


def _diff_attention(qk, v, kv_ctx, lam_vecs, subln_g, *, lam_init, n_heads, bq=2048, q_sub=512, key_chunk=1024,
                    casts=()):
    b, l, d = v.shape
    c = kv_ctx.shape[1]
    bq, key_chunk = min(bq, l), min(key_chunk, l)
    q_sub = min(q_sub, bq)
    hw = V_HEAD_DIM
    assert hw == 2 * DIFF_HEAD_DIM and n_heads * hw == d and l % key_chunk == 0
    hb = d // hw
    grid = (b, n_heads, l // bq)
    cast_in, cast_out, cast_shapes, cast_bytes = _cast_plan(
        casts, grid, lambda bi, h, qi: (bi * grid[1] + h) * grid[2] + qi)
    pipelined = (2 * _nbytes((bq, hw), BF16) + 2 * _nbytes((c, hw), BF16) + 2 * _nbytes((l, hw), BF16)
                 + cast_bytes)
    resident = (bq // q_sub) * (8 * _nbytes((q_sub, key_chunk), F32) + 8 * _nbytes((q_sub, hw), F32))
    in_specs = [
        pl.BlockSpec((4, DIFF_HEAD_DIM), lambda bi, h, qi: (0, 0)),
        pl.BlockSpec((1, hw), lambda bi, h, qi: (0, 0)),
        pl.BlockSpec((None, bq, hw), lambda bi, h, qi: (bi, qi, h)),
        pl.BlockSpec((None, c, hw), lambda bi, h, qi: (bi, 0, h)),
        pl.BlockSpec((None, l, hw), lambda bi, h, qi: (bi, 0, hb + h)),
        pl.BlockSpec((None, c, hw), lambda bi, h, qi: (bi, 0, hb + h)),
        pl.BlockSpec((None, l, hw), lambda bi, h, qi: (bi, 0, h)),
    ]
    outs = pl.pallas_call(
        _hosting_casts(functools.partial(_attn_kernel, lam_init=lam_init, key_chunk=key_chunk, q_sub=q_sub),
                       len(in_specs), 1, len(casts)),
        out_shape=[jax.ShapeDtypeStruct((b, l, d), BF16)] + cast_shapes,
        grid=grid,
        in_specs=in_specs + cast_in,
        out_specs=[pl.BlockSpec((None, bq, hw), lambda bi, h, qi: (bi, qi, h))] + cast_out,
        scratch_shapes=[pltpu.VMEM((V7X_SUBLANES, V7X_LANES), F32)],
        compiler_params=_params(3, _vmem_limit(pipelined, resident)),
        name="diff_attention",
    )(lam_vecs, subln_g.reshape(1, hw), qk, kv_ctx, qk, kv_ctx, v, *[src for src, _ in casts])
    return outs[0] if not casts else tuple(outs)


def _conv_ln_kernel(prev_ref, u_ref, next_ref, w_ref, dwb_ref, lng_ref, lnb_ref, o_ref,
                    ubuf, rbuf, ybuf, *, halo, rb, cb, ln_rows):
    tr, d = u_ref.shape
    rows = tr + 2 * halo
    ti, nt = pl.program_id(1), pl.num_programs(1)
    ubuf[0:halo, :] = jnp.where(ti > 0, prev_ref[...], 0.0)
    ubuf[halo:halo + tr, :] = u_ref[...]
    ubuf[halo + tr:, :] = jnp.where(ti < nt - 1, next_ref[...], 0.0)

    def conv_chunk(ci, carry):
        c0 = pl.multiple_of(ci * cb, cb)
        u_chunk = ubuf[:, pl.ds(c0, cb)]
        for s in range(1, V7X_SUBLANES):
            rbuf[s - 1] = pltpu.roll(u_chunk, rows - s, 0)
        for r0 in range(0, tr, rb):
            acc = jnp.zeros((rb, cb), F32)
            for tap in range(CONV_WIDTH):
                row = halo + r0 + tap - CONV_PAD
                s, base = row % V7X_SUBLANES, row - row % V7X_SUBLANES
                if s == 0:
                    src = ubuf[base:base + rb, pl.ds(c0, cb)]
                else:
                    src = rbuf[s - 1, base:base + rb, :]
                acc = acc + src * w_ref[tap:tap + 1, pl.ds(c0, cb)]
            ybuf[r0:r0 + rb, pl.ds(c0, cb)] = acc + dwb_ref[:, pl.ds(c0, cb)]
        return carry

    lax.fori_loop(0, d // cb, conv_chunk, 0)

    for r0 in range(0, tr, ln_rows):
        y = ybuf[r0:r0 + ln_rows, :]
        mu = jnp.mean(y, axis=-1, keepdims=True)
        yc = y - mu
        var = jnp.mean(yc * yc, axis=-1, keepdims=True)
        z = yc * lax.rsqrt(var + LN_EPS) * lng_ref[...] + lnb_ref[...]
        o_ref[r0:r0 + ln_rows, :] = (z * jax.nn.sigmoid(z)).astype(o_ref.dtype)


def _conv_ln_swish(u, dw_w, dw_b, ln_g, ln_b, *, tr=256, halo=16, rb=64, cb=512):
    b, l, d = u.shape
    assert halo >= CONV_PAD and halo % V7X_SUBLANES == 0 and tr % halo == 0 and l % tr == 0
    assert halo - CONV_PAD >= 0 and CONV_WIDTH - 1 - CONV_PAD <= halo
    hb = tr // halo
    last_halo_block = l // halo - 1
    w_rows = -(-CONV_WIDTH // V7X_SUBLANES) * V7X_SUBLANES
    w_pad = jnp.zeros((w_rows, d), F32).at[:CONV_WIDTH].set(dw_w)
    vec = lambda v: v.reshape(1, d)
    vec_spec = pl.BlockSpec((1, d), lambda bi, ti: (0, 0))
    rows = tr + 2 * halo
    pipelined = (_nbytes((tr, d), F32) + 2 * _nbytes((halo, d), F32) + _nbytes((tr, d), BF16)
                 + _nbytes((w_rows + 3, d), F32))
    resident = (_nbytes((rows, d), F32) + _nbytes((V7X_SUBLANES - 1, rows, cb), F32) + _nbytes((tr, d), F32)
                + 2 * _nbytes((rows, cb), F32) + 8 * _nbytes((rb, d), F32))
    return pl.pallas_call(
        functools.partial(_conv_ln_kernel, halo=halo, rb=rb, cb=cb, ln_rows=min(64, tr)),
        out_shape=jax.ShapeDtypeStruct((b, l, d), BF16),
        grid=(b, l // tr),
        in_specs=[
            pl.BlockSpec((None, halo, d), lambda bi, ti: (bi, jnp.maximum(ti * hb - 1, 0), 0)),
            pl.BlockSpec((None, tr, d), lambda bi, ti: (bi, ti, 0)),
            pl.BlockSpec((None, halo, d), lambda bi, ti: (bi, jnp.minimum((ti + 1) * hb, last_halo_block), 0)),
            pl.BlockSpec((w_rows, d), lambda bi, ti: (0, 0)),
            vec_spec, vec_spec, vec_spec,
        ],
        out_specs=pl.BlockSpec((None, tr, d), lambda bi, ti: (bi, ti, 0)),
        scratch_shapes=[pltpu.VMEM((rows, d), F32),
                        pltpu.VMEM((V7X_SUBLANES - 1, rows, cb), F32),
                        pltpu.VMEM((tr, d), F32)],
        compiler_params=_params(2, _vmem_limit(pipelined, resident)),
        name="conv_ln_swish",
    )(u, u, u, w_pad, vec(dw_b), vec(ln_g), vec(ln_b))


def _rope_tables(length):
    rows = length // GRID_W
    row = jnp.repeat(jnp.arange(rows, dtype=F32), GRID_W)
    col = jnp.tile(jnp.arange(GRID_W, dtype=F32), rows)
    n = DIFF_HEAD_DIM // 4
    inv = ROPE_BASE ** (-jnp.arange(n, dtype=F32) / n)
    ar, ac = row[:, None] * inv, col[:, None] * inv
    ang = jnp.concatenate([ar, ar, ac, ac], axis=-1)
    cos, sin = jnp.cos(ang), jnp.sin(ang)
    first_half = (jnp.arange(DIFF_HEAD_DIM) % (2 * n)) < n
    sin_lo = jnp.where(first_half, -sin, 0.0)
    sin_hi = jnp.where(first_half, 0.0, sin)
    q_scale = ATTN_SCALE * math.log2(math.e)
    return (jnp.stack([cos * q_scale, cos]),
            jnp.stack([sin_lo * q_scale, sin_lo]),
            jnp.stack([sin_hi * q_scale, sin_hi]))


def _mlp(h, w_up, w_down_f32, layer, x, gate, seq_len, *, extra_casts=()):
    n_ff = w_up.shape[1]
    hid, w_down, *hosted = _mm_call(
        _mm_relu2_kernel, h, w_up[None], 0, n_ff, out_dtype=BF16,
        bm=min(MM_BM, h.shape[0]), bn=min(MM_BN, n_ff),
        casts=[(w_down_f32, layer), *extra_casts], name=f"mlp_up_{layer}")
    x_new = _mm_resid(hid, w_down[None], 0, x, gate, seq_len, bk=min(MLP_DOWN_BK, n_ff),
                      name=f"mlp_down_{layer}")
    return x_new, hosted


def kernel(x, c, ctx, c_ctx, ada_w, ada_b, norm_mix_g, norm_mlp_g, norm_final_g, attn_w_qkv, attn_w_o, lambda_q1, lambda_k1, lambda_q2, lambda_k2, attn_subln_g, conv_pw1_w, conv_pw1_b, conv_dw_w, conv_dw_b, conv_ln_g, conv_ln_b, conv_pw2_w, conv_pw2_b, mlp_w1, mlp_w2):
    b, l, d = x.shape
    n_ctx = ctx.shape[1]
    depth = ada_w.shape[0]
    assert depth == 2 and attn_w_qkv.shape[0] == 1 and conv_pw1_w.shape[0] == 1
    n_heads = d // V_HEAD_DIM
    m = b * l

    rows = -(-(b + 1) // V7X_SUBLANES) * V7X_SUBLANES
    cvec = jnp.zeros((rows, d), F32).at[:b].set(c).at[b].set(c_ctx)
    mods = _adaln(cvec, ada_w, ada_b)

    def lat_mod(layer, j):
        return mods[layer, :b, j * d:(j + 1) * d].reshape(b, 1, d)

    def ctx_mod(layer, j):
        return mods[layer, b:b + 1, j * d:(j + 1) * d].reshape(1, 1, d)

    w_qkv = attn_w_qkv.astype(BF16)
    w_pw1 = None

    x_lat = x
    for layer in range(depth):
        kind, idx = layer % N_MIXERS, layer // N_MIXERS
        sh1, sc1, g1, sh2, sc2, g2 = (lat_mod(layer, j) for j in range(6))
        h_lat = _norm(x_lat, norm_mix_g[layer], sh1, sc1, out_dtype=BF16).reshape(m, d)
        x_flat = x_lat.reshape(m, d)
        bm = min(MM_BM, l)
        if kind == 0:
            lam_init = 0.8 - 0.6 * math.exp(-0.3 * layer)
            h_ctx = _norm(ctx, norm_mix_g[layer], ctx_mod(layer, 0), ctx_mod(layer, 1), out_dtype=BF16)
            bn = min(MM_BN, d)
            qk, v = _qkv_proj(h_lat, w_qkv, idx, l, _rope_tables(l), bm=bm, bn_qk=bn)
            kv_ctx = _mm_call(
                _mm_plain_kernel, h_ctx.reshape(b * n_ctx, d), w_qkv, idx, 2 * d,
                w_col_offsets=(d,), out_dtype=BF16, bm=min(MM_BM, b * n_ctx), bn=bn, name="kv_ctx")
            lam_vecs = jnp.stack([lambda_q1[idx], lambda_k1[idx], lambda_q2[idx], lambda_k2[idx]])
            o, w_o, w_up = _diff_attention(
                qk.reshape(b, l, 2 * d), v.reshape(b, l, d), kv_ctx.reshape(b, n_ctx, 2 * d),
                lam_vecs, attn_subln_g[idx], lam_init=lam_init, n_heads=n_heads,
                casts=[(attn_w_o, idx), (mlp_w1, layer)])
            x_flat = _mm_resid(o.reshape(m, d), w_o[None], 0, x_flat, g1, l, name="attn_out")
            mlp_hosted = [(conv_pw1_w, 0)]
        else:
            bn = min(GLU_BN, d)
            bias = conv_pw1_b[idx].reshape(1, 2 * d)
            u, w_pw2, w_up = _mm_call(
                _mm_glu_kernel, h_lat, w_pw1[None], 0, d, w_col_offsets=(0, d),
                extras=[(bias, pl.BlockSpec((1, bn), lambda i, j, k: (0, j))),
                        (bias, pl.BlockSpec((1, bn), lambda i, j, k: (0, j + d // bn)))],
                out_dtype=F32, bm=bm, bn=bn, n_temps=6,
                casts=[(conv_pw2_w, idx), (mlp_w1, layer)], name="conv_pw1_glu")
            s = _conv_ln_swish(u.reshape(b, l, d), conv_dw_w[idx], conv_dw_b[idx],
                               conv_ln_g[idx], conv_ln_b[idx])
            x_flat = _mm_resid(s.reshape(m, d), w_pw2[None], 0, x_flat, g1, l,
                               bias=conv_pw2_b[idx], name="conv_pw2")
            mlp_hosted = []
        h2 = _norm(x_flat.reshape(b, l, d), norm_mlp_g[layer], sh2, sc2, out_dtype=BF16).reshape(m, d)
        x_flat, hosted = _mlp(h2, w_up, mlp_w2, layer, x_flat, g2, l, extra_casts=mlp_hosted)
        if hosted:
            w_pw1 = hosted[0]
        x_lat = x_flat.reshape(b, l, d)
    return _norm(x_lat, norm_final_g, out_dtype=F32)
```

```python
import functools
import math
from typing import NamedTuple

import jax
import jax.numpy as jnp
from jax import lax
from jax.experimental import pallas as pl
from jax.experimental.pallas import tpu as pltpu

GRID_W = 64
DIFF_HEAD_DIM = 128
V_HEAD_DIM = 2 * DIFF_HEAD_DIM
ATTN_SCALE = DIFF_HEAD_DIM ** -0.5
ROPE_BASE = 10000.0
CONV_WIDTH = 31
CONV_PAD = CONV_WIDTH // 2
RMS_EPS = 1e-6
LN_EPS = 1e-5
N_MIXERS = 2

V7X_LANES = 128
V7X_SUBLANES = 8
V7X_VMEM_LIMIT_CAP = 60000 * 1024

MM_BM = 1024
MM_BN = 1024
MLP_DOWN_BK = 4096
GLU_BN = 512

F32 = jnp.float32
BF16 = jnp.bfloat16


def _nbytes(shape, dtype):
    return math.prod(shape) * jnp.dtype(dtype).itemsize


def _vmem_limit(pipelined_bytes, resident_bytes):
    return int(min(V7X_VMEM_LIMIT_CAP, 2 * pipelined_bytes + resident_bytes))


def _params(n_axes, vmem_bytes):
    return pltpu.CompilerParams(
        dimension_semantics=("arbitrary",) * n_axes, vmem_limit_bytes=vmem_bytes)


def _adaln_kernel(c_ref, w_ref, b_ref, o_ref):
    c = c_ref[...]
    s = (c * jax.nn.sigmoid(c)).astype(BF16)
    acc = jnp.dot(s, w_ref[...].astype(BF16), preferred_element_type=F32)
    o_ref[...] = acc + b_ref[...]


def _adaln(cvec, ada_w, ada_b, *, bn=512):
    depth, d, n = ada_w.shape
    rows = cvec.shape[0]
    pipelined = _nbytes((d, bn), F32) + _nbytes((rows, bn), F32) + _nbytes((1, bn), F32)
    resident = 2 * _nbytes((rows, d), F32) + _nbytes((d, bn), BF16)
    return pl.pallas_call(
        _adaln_kernel,
        out_shape=jax.ShapeDtypeStruct((depth, rows, n), F32),
        grid=(depth, n // bn),
        in_specs=[
            pl.BlockSpec((rows, d), lambda l, j: (0, 0)),
            pl.BlockSpec((None, d, bn), lambda l, j: (l, 0, j)),
            pl.BlockSpec((None, 1, bn), lambda l, j: (l, 0, j)),
        ],
        out_specs=pl.BlockSpec((None, rows, bn), lambda l, j: (l, 0, j)),
        compiler_params=_params(2, _vmem_limit(pipelined, resident)),
        name="adaln",
    )(cvec, ada_w, ada_b.reshape(depth, 1, n))


def _norm_kernel(x_ref, g_ref, *rest, modulate, rb):
    o_ref = rest[-1]
    x = x_ref[...]
    rstd = lax.rsqrt(jnp.mean(x * x, axis=-1, keepdims=True) + RMS_EPS)
    gain = g_ref[...]
    if modulate:
        sh_ref, sc_ref = rest[0], rest[1]
        gain = gain * (1.0 + sc_ref[...])
    for r0 in range(0, x_ref.shape[0], rb):
        y = (x_ref[r0:r0 + rb, :] * rstd[r0:r0 + rb]) * gain
        if modulate:
            y = y + sh_ref[...]
        o_ref[r0:r0 + rb, :] = y.astype(o_ref.dtype)


def _norm(x, g, shift=None, scale=None, *, out_dtype, tr=512):
    b, t, d = x.shape
    tr = min(tr, t)
    modulate = shift is not None
    operands = [x, g.reshape(1, d)]
    in_specs = [
        pl.BlockSpec((None, tr, d), lambda bi, ti: (bi, ti, 0)),
        pl.BlockSpec((1, d), lambda bi, ti: (0, 0)),
    ]
    if modulate:
        per_batch = shift.shape[0] == b
        vec_map = (lambda bi, ti: (bi, 0, 0)) if per_batch else (lambda bi, ti: (0, 0, 0))
        operands += [shift, scale]
        in_specs += [pl.BlockSpec((None, 1, d), vec_map)] * 2
    pipelined = _nbytes((tr, d), F32) + _nbytes((tr, d), out_dtype) + 3 * _nbytes((1, d), F32)
    resident = 3 * _nbytes((tr, d), F32)
    return pl.pallas_call(
        functools.partial(_norm_kernel, modulate=modulate, rb=min(32, tr)),
        out_shape=jax.ShapeDtypeStruct((b, t, d), out_dtype),
        grid=(b, t // tr),
        in_specs=in_specs,
        out_specs=pl.BlockSpec((None, tr, d), lambda bi, ti: (bi, ti, 0)),
        compiler_params=_params(2, _vmem_limit(pipelined, resident)),
        name="rmsnorm_mod" if modulate else "rmsnorm",
    )(*operands)


def _cast_plan(jobs, grid, step_of):
    n_steps = math.prod(grid)
    in_specs, out_specs, out_shapes, tile_bytes = [], [], [], 0
    for src, layer in jobs:
        _, kdim, n = src.shape
        rows = kdim // n_steps
        assert rows * n_steps == kdim and rows % (2 * V7X_SUBLANES) == 0, (src.shape, n_steps)
        in_specs.append(pl.BlockSpec((None, rows, n), lambda *ids, layer=layer: (layer, step_of(*ids), 0)))
        out_specs.append(pl.BlockSpec((rows, n), lambda *ids: (step_of(*ids), 0)))
        out_shapes.append(jax.ShapeDtypeStruct((kdim, n), BF16))
        tile_bytes += _nbytes((rows, n), F32) + _nbytes((rows, n), BF16)
    return in_specs, out_specs, out_shapes, tile_bytes


def _hosting_casts(body, n_in, n_out, n_casts):
    if n_casts == 0:
        return body

    def wrapped(*refs):
        ins, rest = refs[:n_in], refs[n_in:]
        cast_ins, rest = rest[:n_casts], rest[n_casts:]
        outs, rest = rest[:n_out], rest[n_out:]
        cast_outs, scratch = rest[:n_casts], rest[n_casts:]
        body(*ins, *outs, *scratch)
        for src_ref, dst_ref in zip(cast_ins, cast_outs):
            dst_ref[...] = src_ref[...].astype(dst_ref.dtype)

    return wrapped


class _NormedRows(NamedTuple):
    x: jax.Array
    gain: jax.Array
    shift: jax.Array
    scale: jax.Array
    seq_len: int


def _step_kernel(body, n_in, n_out, n_casts, fused_norm):
    if not fused_norm:
        return _hosting_casts(body, n_in, n_out, n_casts)
    step = _hosting_casts(body, n_in + 1, n_out, n_casts)

    def wrapped(x_ref, g_ref, sh_ref, sc_ref, *refs):
        refs, (h_even, h_odd) = refs[:-2], refs[-2:]
        i, j = pl.program_id(0), pl.program_id(1)
        slab = x_ref.shape[0]

        def normalise_slab_into(dst):
            x = x_ref[...]
            rstd = lax.rsqrt(jnp.mean(x * x, axis=-1, keepdims=True) + RMS_EPS)
            y = (x * rstd) * (g_ref[...] * (1.0 + sc_ref[...])) + sh_ref[...]
            dst[pl.ds(pl.multiple_of(j * slab, slab), slab), :] = y.astype(dst.dtype)

        @pl.when(i == 0)
        def _():
            normalise_slab_into(h_even)

        @pl.when(jnp.logical_and(i > 0, i % 2 == 1))
        def _():
            step(h_even, *refs)
            normalise_slab_into(h_odd)

        @pl.when(jnp.logical_and(i > 0, i % 2 == 0))
        def _():
            step(h_odd, *refs)
            normalise_slab_into(h_even)

    return wrapped


def _dot(a_ref, w_ref):
    return jnp.dot(a_ref[...], w_ref[...], preferred_element_type=F32)


def _mm_plain_kernel(a_ref, w_ref, o_ref):
    o_ref[...] = _dot(a_ref, w_ref).astype(o_ref.dtype)


def _qkv_kernel(a_ref, wqk_ref, wv_ref, cos_ref, sin_lo_ref, sin_hi_ref, qk_ref, v_ref):
    a = a_ref[...]
    acc = jnp.dot(a, wqk_ref[...], preferred_element_type=F32)
    cos, sin_lo, sin_hi = cos_ref[...], sin_lo_ref[...], sin_hi_ref[...]
    quarter = DIFF_HEAD_DIM // 4
    for c0 in range(0, acc.shape[1], DIFF_HEAD_DIM):
        xh = acc[:, c0:c0 + DIFF_HEAD_DIM]
        r = (xh * cos
             + pltpu.roll(xh, DIFF_HEAD_DIM - quarter, 1) * sin_lo
             + pltpu.roll(xh, quarter, 1) * sin_hi)
        qk_ref[:, c0:c0 + DIFF_HEAD_DIM] = r.astype(qk_ref.dtype)
    v_ref[...] = jnp.dot(a, wv_ref[...], preferred_element_type=F32).astype(v_ref.dtype)


def _qkv_proj(h, w_qkv, layer, seq_len, tables, *, bm, bn_qk):
    m, d = h.shape
    n_tiles = 2 * d // bn_qk
    bn_v = d // n_tiles
    assert m % bm == 0 and seq_len % bm == 0 and d % bn_qk == 0 and bn_v % V7X_LANES == 0
    tab_spec = pl.BlockSpec((None, bm, DIFF_HEAD_DIM),
                            lambda i, j: (j // (d // bn_qk), i % (seq_len // bm), 0))
    pipelined = (_nbytes((bm, d), BF16) + _nbytes((d, bn_qk + bn_v), BF16) + _nbytes((bm, bn_qk + bn_v), BF16)
                 + 3 * _nbytes((bm, DIFF_HEAD_DIM), F32))
    resident = 3 * _nbytes((bm, bn_qk), F32)
    return pl.pallas_call(
        _qkv_kernel,
        out_shape=(jax.ShapeDtypeStruct((m, 2 * d), BF16), jax.ShapeDtypeStruct((m, d), BF16)),
        grid=(m // bm, n_tiles),
        in_specs=[
            pl.BlockSpec((bm, d), lambda i, j: (i, 0)),
            pl.BlockSpec((None, d, bn_qk), lambda i, j: (layer, 0, j)),
            pl.BlockSpec((None, d, bn_v), lambda i, j: (layer, 0, 2 * d // bn_v + j)),
            tab_spec, tab_spec, tab_spec,
        ],
        out_specs=(pl.BlockSpec((bm, bn_qk), lambda i, j: (i, j)),
                   pl.BlockSpec((bm, bn_v), lambda i, j: (i, j))),
        compiler_params=_params(2, _vmem_limit(pipelined, resident)),
        name="qkv_rope",
    )(h, w_qkv, w_qkv, *tables)


def _mm_relu2_kernel(a_ref, w_ref, o_ref):
    acc = _dot(a_ref, w_ref)
    r = jnp.maximum(acc, 0.0)
    o_ref[...] = (r * r).astype(o_ref.dtype)


def _mm_glu_kernel(a_ref, wa_ref, wg_ref, ba_ref, bg_ref, o_ref):
    a = a_ref[...]
    half = o_ref.shape[1] // 2
    for c0 in (0, half):
        lin = jnp.dot(a, wa_ref[:, c0:c0 + half], preferred_element_type=F32) + ba_ref[:, c0:c0 + half]
        gate = jnp.dot(a, wg_ref[:, c0:c0 + half], preferred_element_type=F32) + bg_ref[:, c0:c0 + half]
        o_ref[:, c0:c0 + half] = (lin * jax.nn.sigmoid(gate)).astype(o_ref.dtype)


def _mm_resid_kernel(a_ref, w_ref, x_ref, gate_ref, *rest, nk, has_bias):
    bias_ref, o_ref = (rest[0], rest[1]) if has_bias else (None, rest[0])
    if nk == 1:
        y = _dot(a_ref, w_ref)
        if has_bias:
            y = y + bias_ref[...]
        o_ref[...] = x_ref[...] + gate_ref[...] * y
        return

    @pl.when(pl.program_id(2) == 0)
    def _():
        x0 = x_ref[...]
        o_ref[...] = x0 + gate_ref[...] * bias_ref[...] if has_bias else x0

    o_ref[...] += gate_ref[...] * _dot(a_ref, w_ref)


def _mm_call(body, a, w, layer, n_cols, *, w_col_offsets=(0,), extras=(), out_dtype, bm, bn, bk=None,
             extra_tile_bytes=0, n_temps=3, casts=(), name):
    fused = isinstance(a, _NormedRows)
    m, kdim = a.x.shape if fused else a.shape
    bk = kdim if bk is None else bk
    nk = kdim // bk
    assert m % bm == 0 and n_cols % bn == 0 and kdim % bk == 0, (name, m, kdim, n_cols, bm, bn, bk)
    n_i, n_j = m // bm, n_cols // bn
    if fused:
        assert nk == 1 and bm % n_j == 0 and a.seq_len % bm == 0
        grid = (n_i + 1, n_j, 1)
        row = lambda i: jnp.maximum(i - 1, 0)
        col = lambda i, j: jnp.where(i == 0, 0, j)
    else:
        grid = (n_i, n_j, nk)
        row = lambda i: i
        col = lambda i, j: j

    def remap(index_map):
        return lambda i, j, k: index_map(row(i), col(i, j), k)

    in_specs = [] if fused else [pl.BlockSpec((bm, bk), remap(lambda i, j, k: (i, k)))]
    for off in w_col_offsets:
        assert off % bn == 0
        in_specs.append(pl.BlockSpec((None, bk, bn), remap(lambda i, j, k, ob=off // bn: (layer, k, j + ob))))
    in_specs += [pl.BlockSpec(spec.block_shape, remap(spec.index_map)) for _, spec in extras]
    cast_in, cast_out, cast_shapes, cast_bytes = _cast_plan(
        casts, (n_i, n_j, nk), lambda i, j, k: (row(i) * n_j + col(i, j)) * nk + k)
    pipelined = (len(w_col_offsets) * _nbytes((bk, bn), BF16) + _nbytes((bm, bn), out_dtype)
                 + extra_tile_bytes + cast_bytes)
    resident = n_temps * _nbytes((bm, bn), F32)
    operands, norm_specs, scratch = [], [], []
    if fused:
        slab, last = bm // n_j, n_i - 1
        blocks_per_batch = a.seq_len // bm
        vec_spec = pl.BlockSpec((None, 1, kdim), lambda i, j, k: (jnp.minimum(i, last) // blocks_per_batch, 0, 0))
        operands = [a.x, a.gain.reshape(1, kdim), a.shift, a.scale]
        norm_specs = [pl.BlockSpec((slab, kdim), lambda i, j, k: (jnp.minimum(i, last) * n_j + j, 0)),
                      pl.BlockSpec((1, kdim), lambda i, j, k: (0, 0)), vec_spec, vec_spec]
        scratch = [pltpu.VMEM((bm, kdim), BF16)] * 2
        pipelined += _nbytes((slab, kdim), F32) + 3 * _nbytes((1, kdim), F32)
        resident += 2 * _nbytes((bm, kdim), BF16) + 3 * _nbytes((slab, kdim), F32)
    else:
        operands = [a]
        pipelined += _nbytes((bm, bk), BF16)
    outs = pl.pallas_call(
        _step_kernel(body, len(in_specs), 1, len(casts), fused),
        out_shape=[jax.ShapeDtypeStruct((m, n_cols), out_dtype)] + cast_shapes,
        grid=grid,
        in_specs=norm_specs + in_specs + cast_in,
        out_specs=[pl.BlockSpec((bm, bn), remap(lambda i, j, k: (i, j)))] + cast_out,
        scratch_shapes=scratch,
        compiler_params=_params(3, _vmem_limit(pipelined, resident)),
        name=name,
    )(*operands, *([w] * len(w_col_offsets)), *[arr for arr, _ in extras], *[src for src, _ in casts])
    return outs[0] if not casts else tuple(outs)


def _mm_resid(a, w, layer, x, gate, seq_len, *, bias=None, bk=None, name):
    n = w.shape[2]
    bm, bn = min(MM_BM, seq_len), min(MM_BN, n)
    bk = a.shape[1] if bk is None else bk
    nk = a.shape[1] // bk
    blocks_per_batch = seq_len // bm
    extras = [
        (x, pl.BlockSpec((bm, bn), lambda i, j, k: (i, j))),
        (gate, pl.BlockSpec((None, 1, bn), lambda i, j, k: (i // blocks_per_batch, 0, j))),
    ]
    if bias is not None:
        extras.append((bias.reshape(1, n), pl.BlockSpec((1, bn), lambda i, j, k: (0, j))))
    return _mm_call(
        functools.partial(_mm_resid_kernel, nk=nk, has_bias=bias is not None),
        a, w, layer, n, extras=extras, out_dtype=F32, bm=bm, bn=bn, bk=bk,
        extra_tile_bytes=_nbytes((bm, bn), F32), n_temps=2, name=name)


def _lane_fold_sum(x):
    out = x[:, 0:V7X_LANES]
    for c0 in range(V7X_LANES, x.shape[1], V7X_LANES):
        out = out + x[:, c0:c0 + V7X_LANES]
    return out


def _attn_kernel(lam_vecs_ref, subln_g_ref, q_ref, kc_ref, k_ref, vc_ref, v_ref, o_ref,
                 lam_ref, *, lam_init, key_chunk, q_sub):
    dk = DIFF_HEAD_DIM
    n_ctx, n_lat = kc_ref.shape[0], k_ref.shape[0]

    first = jnp.logical_and(pl.program_id(0) == 0,
                            jnp.logical_and(pl.program_id(1) == 0, pl.program_id(2) == 0))

    @pl.when(first)
    def _():
        lv = lam_vecs_ref[...]
        d1 = jnp.sum(lv[0:1, :] * lv[1:2, :], axis=-1, keepdims=True)
        d2 = jnp.sum(lv[2:3, :] * lv[3:4, :], axis=-1, keepdims=True)
        lam_ref[...] = jnp.broadcast_to(jnp.exp(d1) - jnp.exp(d2) + lam_init, lam_ref.shape)

    groups = [(kc_ref, vc_ref, 0, n_ctx)]
    groups += [(k_ref, v_ref, r0, key_chunk) for r0 in range(0, n_lat, key_chunk)]

    lam = lam_ref[0:1, 0:1]
    for q0 in range(0, q_ref.shape[0], q_sub):
        heads = []
        for comp in range(2):
            qc = q_ref[q0:q0 + q_sub, comp * dk:(comp + 1) * dk]
            m = lsum = acc = None
            for key_ref, val_ref, r0, rows in groups:
                kk = key_ref[r0:r0 + rows, comp * dk:(comp + 1) * dk]
                s = lax.dot_general(qc, kk, (((1,), (1,)), ((), ())), preferred_element_type=F32)
                gmax = jnp.max(s, axis=-1, keepdims=True)
                m_new = gmax if m is None else jnp.maximum(m, gmax)
                e = jnp.exp2(s - m_new)
                esum = _lane_fold_sum(e)
                pv = jnp.dot(e.astype(BF16), val_ref[r0:r0 + rows, :], preferred_element_type=F32)
                if m is None:
                    lsum, acc = esum, pv
                else:
                    alpha = jnp.exp2(m - m_new)
                    lsum, acc = lsum * alpha + esum, acc * alpha + pv
                m = m_new
            heads.append(acc / jnp.sum(lsum, axis=-1, keepdims=True))

        o = heads[0] - lam * heads[1]
        o = o * lax.rsqrt(jnp.mean(o * o, axis=-1, keepdims=True) + RMS_EPS)
        o = (o * subln_g_ref[...]) * (1.0 - lam_init)
        o_ref[q0:q0 + q_sub, :] = o.astype(o_ref.dtype)


def _diff_attention(qk, v, kv_ctx, lam_vecs, subln_g, *, lam_init, n_heads, bq=2048, q_sub=512, key_chunk=1024,
                    casts=()):
    b, l, d = v.shape
    c = kv_ctx.shape[1]
    bq, key_chunk = min(bq, l), min(key_chunk, l)
    q_sub = min(q_sub, bq)
    hw = V_HEAD_DIM
    assert hw == 2 * DIFF_HEAD_DIM and n_heads * hw == d and l % key_chunk == 0
    hb = d // hw
    grid = (b, n_heads, l // bq)
    cast_in, cast_out, cast_shapes, cast_bytes = _cast_plan(
        casts, grid, lambda bi, h, qi: (bi * grid[1] + h) * grid[2] + qi)
    pipelined = (2 * _nbytes((bq, hw), BF16) + 2 * _nbytes((c, hw), BF16) + 2 * _nbytes((l, hw), BF16)
                 + cast_bytes)
    resident = (bq // q_sub) * (8 * _nbytes((q_sub, key_chunk), F32) + 8 * _nbytes((q_sub, hw), F32))
    in_specs = [
        pl.BlockSpec((4, DIFF_HEAD_DIM), lambda bi, h, qi: (0, 0)),
        pl.BlockSpec((1, hw), lambda bi, h, qi: (0, 0)),
        pl.BlockSpec((None, bq, hw), lambda bi, h, qi: (bi, qi, h)),
        pl.BlockSpec((None, c, hw), lambda bi, h, qi: (bi, 0, h)),
        pl.BlockSpec((None, l, hw), lambda bi, h, qi: (bi, 0, hb + h)),
        pl.BlockSpec((None, c, hw), lambda bi, h, qi: (bi, 0, hb + h)),
        pl.BlockSpec((None, l, hw), lambda bi, h, qi: (bi, 0, h)),
    ]
    outs = pl.pallas_call(
        _hosting_casts(functools.partial(_attn_kernel, lam_init=lam_init, key_chunk=key_chunk, q_sub=q_sub),
                       len(in_specs), 1, len(casts)),
        out_shape=[jax.ShapeDtypeStruct((b, l, d), BF16)] + cast_shapes,
        grid=grid,
        in_specs=in_specs + cast_in,
        out_specs=[pl.BlockSpec((None, bq, hw), lambda bi, h, qi: (bi, qi, h))] + cast_out,
        scratch_shapes=[pltpu.VMEM((V7X_SUBLANES, V7X_LANES), F32)],
        compiler_params=_params(3, _vmem_limit(pipelined, resident)),
        name="diff_attention",
    )(lam_vecs, subln_g.reshape(1, hw), qk, kv_ctx, qk, kv_ctx, v, *[src for src, _ in casts])
    return outs[0] if not casts else tuple(outs)


def _conv_ln_kernel(prev_ref, u_ref, next_ref, w_ref, dwb_ref, lng_ref, lnb_ref, o_ref,
                    ubuf, rbuf, ybuf, *, halo, rb, cb, ln_rows):
    tr, d = u_ref.shape
    rows = tr + 2 * halo
    ti, nt = pl.program_id(1), pl.num_programs(1)
    ubuf[0:halo, :] = jnp.where(ti > 0, prev_ref[...], 0.0)
    ubuf[halo:halo + tr, :] = u_ref[...]
    ubuf[halo + tr:, :] = jnp.where(ti < nt - 1, next_ref[...], 0.0)

    def conv_chunk(ci, carry):
        c0 = pl.multiple_of(ci * cb, cb)
        u_chunk = ubuf[:, pl.ds(c0, cb)]
        for s in range(1, V7X_SUBLANES):
            rbuf[s - 1] = pltpu.roll(u_chunk, rows - s, 0)
        for r0 in range(0, tr, rb):
            acc = jnp.zeros((rb, cb), F32)
            for tap in range(CONV_WIDTH):
                row = halo + r0 + tap - CONV_PAD
                s, base = row % V7X_SUBLANES, row - row % V7X_SUBLANES
                if s == 0:
                    src = ubuf[base:base + rb, pl.ds(c0, cb)]
                else:
                    src = rbuf[s - 1, base:base + rb, :]
                acc = acc + src * w_ref[tap:tap + 1, pl.ds(c0, cb)]
            ybuf[r0:r0 + rb, pl.ds(c0, cb)] = acc + dwb_ref[:, pl.ds(c0, cb)]
        return carry

    lax.fori_loop(0, d // cb, conv_chunk, 0)

    for r0 in range(0, tr, ln_rows):
        y = ybuf[r0:r0 + ln_rows, :]
        mu = jnp.mean(y, axis=-1, keepdims=True)
        yc = y - mu
        var = jnp.mean(yc * yc, axis=-1, keepdims=True)
        z = yc * lax.rsqrt(var + LN_EPS) * lng_ref[...] + lnb_ref[...]
        o_ref[r0:r0 + ln_rows, :] = (z * jax.nn.sigmoid(z)).astype(o_ref.dtype)


def _conv_ln_swish(u, dw_w, dw_b, ln_g, ln_b, *, tr=256, halo=16, rb=64, cb=512):
    b, l, d = u.shape
    assert halo >= CONV_PAD and halo % V7X_SUBLANES == 0 and tr % halo == 0 and l % tr == 0
    assert halo - CONV_PAD >= 0 and CONV_WIDTH - 1 - CONV_PAD <= halo
    hb = tr // halo
    last_halo_block = l // halo - 1
    w_rows = -(-CONV_WIDTH // V7X_SUBLANES) * V7X_SUBLANES
    w_pad = jnp.zeros((w_rows, d), F32).at[:CONV_WIDTH].set(dw_w)
    vec = lambda v: v.reshape(1, d)
    vec_spec = pl.BlockSpec((1, d), lambda bi, ti: (0, 0))
    rows = tr + 2 * halo
    pipelined = (_nbytes((tr, d), F32) + 2 * _nbytes((halo, d), F32) + _nbytes((tr, d), BF16)
                 + _nbytes((w_rows + 3, d), F32))
    resident = (_nbytes((rows, d), F32) + _nbytes((V7X_SUBLANES - 1, rows, cb), F32) + _nbytes((tr, d), F32)
                + 2 * _nbytes((rows, cb), F32) + 8 * _nbytes((rb, d), F32))
    return pl.pallas_call(
        functools.partial(_conv_ln_kernel, halo=halo, rb=rb, cb=cb, ln_rows=min(64, tr)),
        out_shape=jax.ShapeDtypeStruct((b, l, d), BF16),
        grid=(b, l // tr),
        in_specs=[
            pl.BlockSpec((None, halo, d), lambda bi, ti: (bi, jnp.maximum(ti * hb - 1, 0), 0)),
            pl.BlockSpec((None, tr, d), lambda bi, ti: (bi, ti, 0)),
            pl.BlockSpec((None, halo, d), lambda bi, ti: (bi, jnp.minimum((ti + 1) * hb, last_halo_block), 0)),
            pl.BlockSpec((w_rows, d), lambda bi, ti: (0, 0)),
            vec_spec, vec_spec, vec_spec,
        ],
        out_specs=pl.BlockSpec((None, tr, d), lambda bi, ti: (bi, ti, 0)),
        scratch_shapes=[pltpu.VMEM((rows, d), F32),
                        pltpu.VMEM((V7X_SUBLANES - 1, rows, cb), F32),
                        pltpu.VMEM((tr, d), F32)],
        compiler_params=_params(2, _vmem_limit(pipelined, resident)),
        name="conv_ln_swish",
    )(u, u, u, w_pad, vec(dw_b), vec(ln_g), vec(ln_b))


def _rope_tables(length):
    rows = length // GRID_W
    row = jnp.repeat(jnp.arange(rows, dtype=F32), GRID_W)
    col = jnp.tile(jnp.arange(GRID_W, dtype=F32), rows)
    n = DIFF_HEAD_DIM // 4
    inv = ROPE_BASE ** (-jnp.arange(n, dtype=F32) / n)
    ar, ac = row[:, None] * inv, col[:, None] * inv
    ang = jnp.concatenate([ar, ar, ac, ac], axis=-1)
    cos, sin = jnp.cos(ang), jnp.sin(ang)
    first_half = (jnp.arange(DIFF_HEAD_DIM) % (2 * n)) < n
    sin_lo = jnp.where(first_half, -sin, 0.0)
    sin_hi = jnp.where(first_half, 0.0, sin)
    q_scale = ATTN_SCALE * math.log2(math.e)
    return (jnp.stack([cos * q_scale, cos]),
            jnp.stack([sin_lo * q_scale, sin_lo]),
            jnp.stack([sin_hi * q_scale, sin_hi]))


def _mlp(h, w_up, w_down_f32, layer, x, gate, seq_len, *, extra_casts=()):
    n_ff = w_up.shape[1]
    hid, w_down, *hosted = _mm_call(
        _mm_relu2_kernel, h, w_up[None], 0, n_ff, out_dtype=BF16,
        bm=min(MM_BM, seq_len), bn=min(MM_BN, n_ff),
        casts=[(w_down_f32, layer), *extra_casts], name=f"mlp_up_{layer}")
    x_new = _mm_resid(hid, w_down[None], 0, x, gate, seq_len, bk=min(MLP_DOWN_BK, n_ff),
                      name=f"mlp_down_{layer}")
    return x_new, hosted


def kernel(x, c, ctx, c_ctx, ada_w, ada_b, norm_mix_g, norm_mlp_g, norm_final_g, attn_w_qkv, attn_w_o, lambda_q1, lambda_k1, lambda_q2, lambda_k2, attn_subln_g, conv_pw1_w, conv_pw1_b, conv_dw_w, conv_dw_b, conv_ln_g, conv_ln_b, conv_pw2_w, conv_pw2_b, mlp_w1, mlp_w2):
    b, l, d = x.shape
    n_ctx = ctx.shape[1]
    depth = ada_w.shape[0]
    assert depth == 2 and attn_w_qkv.shape[0] == 1 and conv_pw1_w.shape[0] == 1
    n_heads = d // V_HEAD_DIM
    m = b * l

    rows = -(-(b + 1) // V7X_SUBLANES) * V7X_SUBLANES
    cvec = jnp.zeros((rows, d), F32).at[:b].set(c).at[b].set(c_ctx)
    mods = _adaln(cvec, ada_w, ada_b)

    def lat_mod(layer, j):
        return mods[layer, :b, j * d:(j + 1) * d].reshape(b, 1, d)

    def ctx_mod(layer, j):
        return mods[layer, b:b + 1, j * d:(j + 1) * d].reshape(1, 1, d)

    w_qkv = attn_w_qkv.astype(BF16)
    w_pw1 = None

    x_lat = x
    for layer in range(depth):
        kind, idx = layer % N_MIXERS, layer // N_MIXERS
        sh1, sc1, g1, sh2, sc2, g2 = (lat_mod(layer, j) for j in range(6))
        x_flat = x_lat.reshape(m, d)
        bm = min(MM_BM, l)
        if kind == 0:
            lam_init = 0.8 - 0.6 * math.exp(-0.3 * layer)
            h_ctx = _norm(ctx, norm_mix_g[layer], ctx_mod(layer, 0), ctx_mod(layer, 1), out_dtype=BF16)
            bn = min(MM_BN, d)
            h_lat = _norm(x_lat, norm_mix_g[layer], sh1, sc1, out_dtype=BF16).reshape(m, d)
            qk, v = _qkv_proj(h_lat, w_qkv, idx, l, _rope_tables(l), bm=bm, bn_qk=bn)
            kv_ctx = _mm_call(
                _mm_plain_kernel, h_ctx.reshape(b * n_ctx, d), w_qkv, idx, 2 * d,
                w_col_offsets=(d,), out_dtype=BF16, bm=min(MM_BM, b * n_ctx), bn=bn, name="kv_ctx")
            lam_vecs = jnp.stack([lambda_q1[idx], lambda_k1[idx], lambda_q2[idx], lambda_k2[idx]])
            o, w_o, w_up = _diff_attention(
                qk.reshape(b, l, 2 * d), v.reshape(b, l, d), kv_ctx.reshape(b, n_ctx, 2 * d),
                lam_vecs, attn_subln_g[idx], lam_init=lam_init, n_heads=n_heads,
                casts=[(attn_w_o, idx), (mlp_w1, layer)])
            x_flat = _mm_resid(o.reshape(m, d), w_o[None], 0, x_flat, g1, l, name="attn_out")
            mlp_hosted = [(conv_pw1_w, 0)]
        else:
            bn = min(GLU_BN, d)
            bias = conv_pw1_b[idx].reshape(1, 2 * d)
            u, w_pw2, w_up = _mm_call(
                _mm_glu_kernel, _NormedRows(x_flat, norm_mix_g[layer], sh1, sc1, l), w_pw1[None], 0, d,
                w_col_offsets=(0, d),
                extras=[(bias, pl.BlockSpec((1, bn), lambda i, j, k: (0, j))),
                        (bias, pl.BlockSpec((1, bn), lambda i, j, k: (0, j + d // bn)))],
                out_dtype=F32, bm=bm, bn=bn, n_temps=6,
                casts=[(conv_pw2_w, idx), (mlp_w1, layer)], name="conv_pw1_glu")
            s = _conv_ln_swish(u.reshape(b, l, d), conv_dw_w[idx], conv_dw_b[idx],
                               conv_ln_g[idx], conv_ln_b[idx])
            x_flat = _mm_resid(s.reshape(m, d), w_pw2[None], 0, x_flat, g1, l,
                               bias=conv_pw2_b[idx], name="conv_pw2")
            mlp_hosted = []
        h2 = _NormedRows(x_flat, norm_mlp_g[layer], sh2, sc2, l)
        x_flat, hosted = _mlp(h2, w_up, mlp_w2, layer, x_flat, g2, l, extra_casts=mlp_hosted)
        if hosted:
            w_pw1 = hosted[0]
        x_lat = x_flat.reshape(b, l, d)
    return _norm(x_lat, norm_final_g, out_dtype=F32)
```

```python
import functools
import math
from typing import NamedTuple

import jax
import jax.numpy as jnp
from jax import lax
from jax.experimental import pallas as pl
from jax.experimental.pallas import tpu as pltpu

GRID_W = 64
DIFF_HEAD_DIM = 128
V_HEAD_DIM = 2 * DIFF_HEAD_DIM
ATTN_SCALE = DIFF_HEAD_DIM ** -0.5
ROPE_BASE = 10000.0
CONV_WIDTH = 31
CONV_PAD = CONV_WIDTH // 2
RMS_EPS = 1e-6
LN_EPS = 1e-5
N_MIXERS = 2

V7X_LANES = 128
V7X_SUBLANES = 8
V7X_VMEM_LIMIT_CAP = 63 * 1024 * 1024

MM_BM = 1024
MM_BN = 1024
MLP_DOWN_BK = 4096
GLU_BN = 512

F32 = jnp.float32
BF16 = jnp.bfloat16


def _nbytes(shape, dtype):
    return math.prod(shape) * jnp.dtype(dtype).itemsize


def _vmem_limit(pipelined_bytes, resident_bytes):
    return int(min(V7X_VMEM_LIMIT_CAP, 2 * pipelined_bytes + resident_bytes))


def _params(n_axes, vmem_bytes):
    return pltpu.CompilerParams(
        dimension_semantics=("arbitrary",) * n_axes, vmem_limit_bytes=vmem_bytes)


def _adaln_kernel(c_ref, w_ref, b_ref, o_ref):
    c = c_ref[...]
    s = (c * jax.nn.sigmoid(c)).astype(BF16)
    acc = jnp.dot(s, w_ref[...].astype(BF16), preferred_element_type=F32)
    o_ref[...] = acc + b_ref[...]


def _adaln(cvec, ada_w, ada_b, *, bn=512):
    depth, d, n = ada_w.shape
    rows = cvec.shape[0]
    pipelined = _nbytes((d, bn), F32) + _nbytes((rows, bn), F32) + _nbytes((1, bn), F32)
    resident = 2 * _nbytes((rows, d), F32) + _nbytes((d, bn), BF16)
    return pl.pallas_call(
        _adaln_kernel,
        out_shape=jax.ShapeDtypeStruct((depth, rows, n), F32),
        grid=(depth, n // bn),
        in_specs=[
            pl.BlockSpec((rows, d), lambda l, j: (0, 0)),
            pl.BlockSpec((None, d, bn), lambda l, j: (l, 0, j)),
            pl.BlockSpec((None, 1, bn), lambda l, j: (l, 0, j)),
        ],
        out_specs=pl.BlockSpec((None, rows, bn), lambda l, j: (l, 0, j)),
        compiler_params=_params(2, _vmem_limit(pipelined, resident)),
        name="adaln",
    )(cvec, ada_w, ada_b.reshape(depth, 1, n))


def _norm_kernel(x_ref, g_ref, *rest, modulate, rb):
    o_ref = rest[-1]
    x = x_ref[...]
    rstd = lax.rsqrt(jnp.mean(x * x, axis=-1, keepdims=True) + RMS_EPS)
    gain = g_ref[...]
    if modulate:
        sh_ref, sc_ref = rest[0], rest[1]
        gain = gain * (1.0 + sc_ref[...])
    for r0 in range(0, x_ref.shape[0], rb):
        y = (x_ref[r0:r0 + rb, :] * rstd[r0:r0 + rb]) * gain
        if modulate:
            y = y + sh_ref[...]
        o_ref[r0:r0 + rb, :] = y.astype(o_ref.dtype)


def _norm(x, g, shift=None, scale=None, *, out_dtype, tr=512):
    b, t, d = x.shape
    tr = min(tr, t)
    modulate = shift is not None
    operands = [x, g.reshape(1, d)]
    in_specs = [
        pl.BlockSpec((None, tr, d), lambda bi, ti: (bi, ti, 0)),
        pl.BlockSpec((1, d), lambda bi, ti: (0, 0)),
    ]
    if modulate:
        per_batch = shift.shape[0] == b
        vec_map = (lambda bi, ti: (bi, 0, 0)) if per_batch else (lambda bi, ti: (0, 0, 0))
        operands += [shift, scale]
        in_specs += [pl.BlockSpec((None, 1, d), vec_map)] * 2
    pipelined = _nbytes((tr, d), F32) + _nbytes((tr, d), out_dtype) + 3 * _nbytes((1, d), F32)
    resident = 3 * _nbytes((tr, d), F32)
    return pl.pallas_call(
        functools.partial(_norm_kernel, modulate=modulate, rb=min(32, tr)),
        out_shape=jax.ShapeDtypeStruct((b, t, d), out_dtype),
        grid=(b, t // tr),
        in_specs=in_specs,
        out_specs=pl.BlockSpec((None, tr, d), lambda bi, ti: (bi, ti, 0)),
        compiler_params=_params(2, _vmem_limit(pipelined, resident)),
        name="rmsnorm_mod" if modulate else "rmsnorm",
    )(*operands)


def _cast_plan(jobs, grid, step_of):
    n_steps = math.prod(grid)
    in_specs, out_specs, out_shapes, tile_bytes = [], [], [], 0
    for src, layer in jobs:
        _, kdim, n = src.shape
        rows = kdim // n_steps
        assert rows * n_steps == kdim and rows % (2 * V7X_SUBLANES) == 0, (src.shape, n_steps)
        in_specs.append(pl.BlockSpec((None, rows, n), lambda *ids, layer=layer: (layer, step_of(*ids), 0)))
        out_specs.append(pl.BlockSpec((rows, n), lambda *ids: (step_of(*ids), 0)))
        out_shapes.append(jax.ShapeDtypeStruct((kdim, n), BF16))
        tile_bytes += _nbytes((rows, n), F32) + _nbytes((rows, n), BF16)
    return in_specs, out_specs, out_shapes, tile_bytes


def _hosting_casts(body, n_in, n_out, n_casts):
    if n_casts == 0:
        return body

    def wrapped(*refs):
        ins, rest = refs[:n_in], refs[n_in:]
        cast_ins, rest = rest[:n_casts], rest[n_casts:]
        outs, rest = rest[:n_out], rest[n_out:]
        cast_outs, scratch = rest[:n_casts], rest[n_casts:]
        body(*ins, *outs, *scratch)
        for src_ref, dst_ref in zip(cast_ins, cast_outs):
            dst_ref[...] = src_ref[...].astype(dst_ref.dtype)

    return wrapped


class _NormedRows(NamedTuple):
    x: jax.Array
    gain: jax.Array
    shift: jax.Array
    scale: jax.Array
    seq_len: int


def _step_kernel(body, n_in, n_out, n_casts, fused_norm):
    if not fused_norm:
        return _hosting_casts(body, n_in, n_out, n_casts)
    step = _hosting_casts(body, n_in + 1, n_out, n_casts)

    def wrapped(x_ref, g_ref, sh_ref, sc_ref, *refs):
        refs, (h_even, h_odd) = refs[:-2], refs[-2:]
        i, j = pl.program_id(0), pl.program_id(1)
        slab = x_ref.shape[0]

        def normalise_slab_into(dst):
            x = x_ref[...]
            rstd = lax.rsqrt(jnp.mean(x * x, axis=-1, keepdims=True) + RMS_EPS)
            y = (x * rstd) * (g_ref[...] * (1.0 + sc_ref[...])) + sh_ref[...]
            dst[pl.ds(pl.multiple_of(j * slab, slab), slab), :] = y.astype(dst.dtype)

        @pl.when(i == 0)
        def _():
            normalise_slab_into(h_even)

        @pl.when(jnp.logical_and(i > 0, i % 2 == 1))
        def _():
            normalise_slab_into(h_odd)
            step(h_even, *refs)

        @pl.when(jnp.logical_and(i > 0, i % 2 == 0))
        def _():
            normalise_slab_into(h_even)
            step(h_odd, *refs)

    return wrapped


def _dot(a_ref, w_ref):
    return jnp.dot(a_ref[...], w_ref[...], preferred_element_type=F32)


def _mm_plain_kernel(a_ref, w_ref, o_ref):
    o_ref[...] = _dot(a_ref, w_ref).astype(o_ref.dtype)


def _qkv_kernel(a_ref, wqk_ref, wv_ref, cos_ref, sin_lo_ref, sin_hi_ref, qk_ref, v_ref):
    a = a_ref[...]
    acc = jnp.dot(a, wqk_ref[...], preferred_element_type=F32)
    cos, sin_lo, sin_hi = cos_ref[...], sin_lo_ref[...], sin_hi_ref[...]
    quarter = DIFF_HEAD_DIM // 4
    for c0 in range(0, acc.shape[1], DIFF_HEAD_DIM):
        xh = acc[:, c0:c0 + DIFF_HEAD_DIM]
        r = (xh * cos
             + pltpu.roll(xh, DIFF_HEAD_DIM - quarter, 1) * sin_lo
             + pltpu.roll(xh, quarter, 1) * sin_hi)
        qk_ref[:, c0:c0 + DIFF_HEAD_DIM] = r.astype(qk_ref.dtype)
    v_ref[...] = jnp.dot(a, wv_ref[...], preferred_element_type=F32).astype(v_ref.dtype)


def _qkv_proj(a, w_qkv, layer, tables, *, bm, bn_qk):
    m, d = a.x.shape
    n_i, n_j = m // bm, 2 * d // bn_qk
    bn_v = d // n_j
    assert m % bm == 0 and a.seq_len % bm == 0 and d % bn_qk == 0 and bn_v % V7X_LANES == 0 and bm % n_j == 0
    slab, last = bm // n_j, n_i - 1
    blocks_per_batch = a.seq_len // bm
    row = lambda i: jnp.maximum(i - 1, 0)
    col = lambda i, j: jnp.where(i == 0, 0, j)
    tab_spec = pl.BlockSpec((None, bm, DIFF_HEAD_DIM),
                            lambda i, j: (col(i, j) // (d // bn_qk), row(i) % blocks_per_batch, 0))
    vec_spec = pl.BlockSpec((None, 1, d), lambda i, j: (jnp.minimum(i, last) // blocks_per_batch, 0, 0))
    pipelined = (_nbytes((d, bn_qk + bn_v), BF16) + _nbytes((bm, bn_qk + bn_v), BF16)
                 + 3 * _nbytes((bm, DIFF_HEAD_DIM), F32) + _nbytes((slab, d), F32) + 3 * _nbytes((1, d), F32))
    resident = 3 * _nbytes((bm, bn_qk), F32) + 2 * _nbytes((bm, d), BF16) + 3 * _nbytes((slab, d), F32)
    return pl.pallas_call(
        _step_kernel(_qkv_kernel, 5, 2, 0, True),
        out_shape=(jax.ShapeDtypeStruct((m, 2 * d), BF16), jax.ShapeDtypeStruct((m, d), BF16)),
        grid=(n_i + 1, n_j),
        in_specs=[
            pl.BlockSpec((slab, d), lambda i, j: (jnp.minimum(i, last) * n_j + j, 0)),
            pl.BlockSpec((1, d), lambda i, j: (0, 0)), vec_spec, vec_spec,
            pl.BlockSpec((None, d, bn_qk), lambda i, j: (layer, 0, col(i, j))),
            pl.BlockSpec((None, d, bn_v), lambda i, j: (layer, 0, 2 * d // bn_v + col(i, j))),
            tab_spec, tab_spec, tab_spec,
        ],
        out_specs=(pl.BlockSpec((bm, bn_qk), lambda i, j: (row(i), col(i, j))),
                   pl.BlockSpec((bm, bn_v), lambda i, j: (row(i), col(i, j)))),
        scratch_shapes=[pltpu.VMEM((bm, d), BF16)] * 2,
        compiler_params=_params(2, _vmem_limit(pipelined, resident)),
        name="qkv_rope",
    )(a.x, a.gain.reshape(1, d), a.shift, a.scale, w_qkv, w_qkv, *tables)


def _mm_relu2_kernel(a_ref, w_ref, o_ref):
    acc = _dot(a_ref, w_ref)
    r = jnp.maximum(acc, 0.0)
    o_ref[...] = (r * r).astype(o_ref.dtype)


def _mm_glu_kernel(a_ref, wa_ref, wg_ref, ba_ref, bg_ref, o_ref):
    a = a_ref[...]
    half = o_ref.shape[1] // 2
    for c0 in (0, half):
        lin = jnp.dot(a, wa_ref[:, c0:c0 + half], preferred_element_type=F32) + ba_ref[:, c0:c0 + half]
        gate = jnp.dot(a, wg_ref[:, c0:c0 + half], preferred_element_type=F32) + bg_ref[:, c0:c0 + half]
        o_ref[:, c0:c0 + half] = (lin * jax.nn.sigmoid(gate)).astype(o_ref.dtype)


def _mm_resid_kernel(a_ref, w_ref, x_ref, gate_ref, *rest, nk, has_bias):
    bias_ref, o_ref = (rest[0], rest[1]) if has_bias else (None, rest[0])
    if nk == 1:
        y = _dot(a_ref, w_ref)
        if has_bias:
            y = y + bias_ref[...]
        o_ref[...] = x_ref[...] + gate_ref[...] * y
        return

    @pl.when(pl.program_id(2) == 0)
    def _():
        x0 = x_ref[...]
        o_ref[...] = x0 + gate_ref[...] * bias_ref[...] if has_bias else x0

    o_ref[...] += gate_ref[...] * _dot(a_ref, w_ref)


def _mm_call(body, a, w, layer, n_cols, *, w_col_offsets=(0,), extras=(), out_dtype, bm, bn, bk=None,
             extra_tile_bytes=0, n_temps=3, casts=(), name):
    fused = isinstance(a, _NormedRows)
    m, kdim = a.x.shape if fused else a.shape
    bk = kdim if bk is None else bk
    nk = kdim // bk
    assert m % bm == 0 and n_cols % bn == 0 and kdim % bk == 0, (name, m, kdim, n_cols, bm, bn, bk)
    n_i, n_j = m // bm, n_cols // bn
    if fused:
        assert nk == 1 and bm % n_j == 0 and a.seq_len % bm == 0
        grid = (n_i + 1, n_j, 1)
        row = lambda i: jnp.maximum(i - 1, 0)
        col = lambda i, j: jnp.where(i == 0, 0, j)
    else:
        grid = (n_i, n_j, nk)
        row = lambda i: i
        col = lambda i, j: j

    def remap(index_map):
        return lambda i, j, k: index_map(row(i), col(i, j), k)

    in_specs = [] if fused else [pl.BlockSpec((bm, bk), remap(lambda i, j, k: (i, k)))]
    for off in w_col_offsets:
        assert off % bn == 0
        in_specs.append(pl.BlockSpec((None, bk, bn), remap(lambda i, j, k, ob=off // bn: (layer, k, j + ob))))
    in_specs += [pl.BlockSpec(spec.block_shape, remap(spec.index_map)) for _, spec in extras]
    cast_in, cast_out, cast_shapes, cast_bytes = _cast_plan(
        casts, (n_i, n_j, nk), lambda i, j, k: (row(i) * n_j + col(i, j)) * nk + k)
    pipelined = (len(w_col_offsets) * _nbytes((bk, bn), BF16) + _nbytes((bm, bn), out_dtype)
                 + extra_tile_bytes + cast_bytes)
    resident = n_temps * _nbytes((bm, bn), F32)
    operands, norm_specs, scratch = [], [], []
    if fused:
        slab, last = bm // n_j, n_i - 1
        blocks_per_batch = a.seq_len // bm
        vec_spec = pl.BlockSpec((None, 1, kdim), lambda i, j, k: (jnp.minimum(i, last) // blocks_per_batch, 0, 0))
        operands = [a.x, a.gain.reshape(1, kdim), a.shift, a.scale]
        norm_specs = [pl.BlockSpec((slab, kdim), lambda i, j, k: (jnp.minimum(i, last) * n_j + j, 0)),
                      pl.BlockSpec((1, kdim), lambda i, j, k: (0, 0)), vec_spec, vec_spec]
        scratch = [pltpu.VMEM((bm, kdim), BF16)] * 2
        pipelined += _nbytes((slab, kdim), F32) + 3 * _nbytes((1, kdim), F32)
        resident += 2 * _nbytes((bm, kdim), BF16) + 3 * _nbytes((slab, kdim), F32)
    else:
        operands = [a]
        pipelined += _nbytes((bm, bk), BF16)
    outs = pl.pallas_call(
        _step_kernel(body, len(in_specs), 1, len(casts), fused),
        out_shape=[jax.ShapeDtypeStruct((m, n_cols), out_dtype)] + cast_shapes,
        grid=grid,
        in_specs=norm_specs + in_specs + cast_in,
        out_specs=[pl.BlockSpec((bm, bn), remap(lambda i, j, k: (i, j)))] + cast_out,
        scratch_shapes=scratch,
        compiler_params=_params(3, _vmem_limit(pipelined, resident)),
        name=name,
    )(*operands, *([w] * len(w_col_offsets)), *[arr for arr, _ in extras], *[src for src, _ in casts])
    return outs[0] if not casts else tuple(outs)


def _mm_resid(a, w, layer, x, gate, seq_len, *, bias=None, bk=None, name):
    n = w.shape[2]
    bm, bn = min(MM_BM, seq_len), min(MM_BN, n)
    bk = a.shape[1] if bk is None else bk
    nk = a.shape[1] // bk
    blocks_per_batch = seq_len // bm
    extras = [
        (x, pl.BlockSpec((bm, bn), lambda i, j, k: (i, j))),
        (gate, pl.BlockSpec((None, 1, bn), lambda i, j, k: (i // blocks_per_batch, 0, j))),
    ]
    if bias is not None:
        extras.append((bias.reshape(1, n), pl.BlockSpec((1, bn), lambda i, j, k: (0, j))))
    return _mm_call(
        functools.partial(_mm_resid_kernel, nk=nk, has_bias=bias is not None),
        a, w, layer, n, extras=extras, out_dtype=F32, bm=bm, bn=bn, bk=bk,
        extra_tile_bytes=_nbytes((bm, bn), F32), n_temps=2, name=name)


def _lane_fold_sum(x):
    out = x[:, 0:V7X_LANES]
    for c0 in range(V7X_LANES, x.shape[1], V7X_LANES):
        out = out + x[:, c0:c0 + V7X_LANES]
    return out


def _attn_kernel(lam_vecs_ref, subln_g_ref, q_ref, kc_ref, k_ref, vc_ref, v_ref, o_ref,
                 lam_ref, *, lam_init, key_chunk, q_sub):
    dk = DIFF_HEAD_DIM
    n_ctx, n_lat = kc_ref.shape[0], k_ref.shape[0]

    first = jnp.logical_and(pl.program_id(0) == 0,
                            jnp.logical_and(pl.program_id(1) == 0, pl.program_id(2) == 0))

    @pl.when(first)
    def _():
        lv = lam_vecs_ref[...]
        d1 = jnp.sum(lv[0:1, :] * lv[1:2, :], axis=-1, keepdims=True)
        d2 = jnp.sum(lv[2:3, :] * lv[3:4, :], axis=-1, keepdims=True)
        lam_ref[...] = jnp.broadcast_to(jnp.exp(d1) - jnp.exp(d2) + lam_init, lam_ref.shape)

    groups = [(kc_ref, vc_ref, 0, n_ctx)]
    groups += [(k_ref, v_ref, r0, key_chunk) for r0 in range(0, n_lat, key_chunk)]

    lam = lam_ref[0:1, 0:1]
    hw = V_HEAD_DIM
    for h0 in range(0, q_ref.shape[1], hw):
        for q0 in range(0, q_ref.shape[0], q_sub):
            heads = []
            for comp in range(2):
                c0 = h0 + comp * dk
                qc = q_ref[q0:q0 + q_sub, c0:c0 + dk]
                m = lsum = acc = None
                for key_ref, val_ref, r0, rows in groups:
                    kk = key_ref[r0:r0 + rows, c0:c0 + dk]
                    s = lax.dot_general(qc, kk, (((1,), (1,)), ((), ())), preferred_element_type=F32)
                    gmax = jnp.max(s, axis=-1, keepdims=True)
                    m_new = gmax if m is None else jnp.maximum(m, gmax)
                    e = jnp.exp2(s - m_new)
                    esum = _lane_fold_sum(e)
                    pv = jnp.dot(e.astype(BF16), val_ref[r0:r0 + rows, h0:h0 + hw], preferred_element_type=F32)
                    if m is None:
                        lsum, acc = esum, pv
                    else:
                        alpha = jnp.exp2(m - m_new)
                        lsum, acc = lsum * alpha + esum, acc * alpha + pv
                    m = m_new
                heads.append(acc / jnp.sum(lsum, axis=-1, keepdims=True))

            o = heads[0] - lam * heads[1]
            o = o * lax.rsqrt(jnp.mean(o * o, axis=-1, keepdims=True) + RMS_EPS)
            o = (o * subln_g_ref[...]) * (1.0 - lam_init)
            o_ref[q0:q0 + q_sub, h0:h0 + hw] = o.astype(o_ref.dtype)


def _diff_attention(qk, v, kv_ctx, lam_vecs, subln_g, *, lam_init, n_heads, bq=2048, q_sub=512, key_chunk=1024,
                    heads_per_step=1, casts=()):
    b, l, d = v.shape
    c = kv_ctx.shape[1]
    bq, key_chunk = min(bq, l), min(key_chunk, l)
    q_sub = min(q_sub, bq)
    hw = V_HEAD_DIM
    assert hw == 2 * DIFF_HEAD_DIM and n_heads * hw == d and l % key_chunk == 0 and n_heads % heads_per_step == 0
    bw = heads_per_step * hw
    hb = d // bw
    grid = (b, n_heads // heads_per_step, l // bq)
    cast_in, cast_out, cast_shapes, cast_bytes = _cast_plan(
        casts, grid, lambda bi, h, qi: (bi * grid[1] + h) * grid[2] + qi)
    pipelined = (2 * _nbytes((bq, bw), BF16) + 2 * _nbytes((c, bw), BF16) + 2 * _nbytes((l, bw), BF16)
                 + cast_bytes)
    resident = (heads_per_step * (bq // q_sub)
                * (4 * _nbytes((q_sub, key_chunk), F32) + 8 * _nbytes((q_sub, hw), F32)))
    in_specs = [
        pl.BlockSpec((4, DIFF_HEAD_DIM), lambda bi, h, qi: (0, 0)),
        pl.BlockSpec((1, hw), lambda bi, h, qi: (0, 0)),
        pl.BlockSpec((None, bq, bw), lambda bi, h, qi: (bi, qi, h)),
        pl.BlockSpec((None, c, bw), lambda bi, h, qi: (bi, 0, h)),
        pl.BlockSpec((None, l, bw), lambda bi, h, qi: (bi, 0, hb + h)),
        pl.BlockSpec((None, c, bw), lambda bi, h, qi: (bi, 0, hb + h)),
        pl.BlockSpec((None, l, bw), lambda bi, h, qi: (bi, 0, h)),
    ]
    outs = pl.pallas_call(
        _hosting_casts(functools.partial(_attn_kernel, lam_init=lam_init, key_chunk=key_chunk, q_sub=q_sub),
                       len(in_specs), 1, len(casts)),
        out_shape=[jax.ShapeDtypeStruct((b, l, d), BF16)] + cast_shapes,
        grid=grid,
        in_specs=in_specs + cast_in,
        out_specs=[pl.BlockSpec((None, bq, bw), lambda bi, h, qi: (bi, qi, h))] + cast_out,
        scratch_shapes=[pltpu.VMEM((V7X_SUBLANES, V7X_LANES), F32)],
        compiler_params=_params(3, _vmem_limit(pipelined, resident)),
        name="diff_attention",
    )(lam_vecs, subln_g.reshape(1, hw), qk, kv_ctx, qk, kv_ctx, v, *[src for src, _ in casts])
    return outs[0] if not casts else tuple(outs)


def _conv_ln_kernel(prev_ref, u_ref, next_ref, w_ref, dwb_ref, lng_ref, lnb_ref, o_ref,
                    ubuf, rbuf, ybuf, *, halo, rb, cb, ln_rows):
    tr, d = u_ref.shape
    rows = tr + 2 * halo
    ti, nt = pl.program_id(1), pl.num_programs(1)
    ubuf[0:halo, :] = jnp.where(ti > 0, prev_ref[...], 0.0)
    ubuf[halo:halo + tr, :] = u_ref[...]
    ubuf[halo + tr:, :] = jnp.where(ti < nt - 1, next_ref[...], 0.0)

    def conv_chunk(ci, carry):
        c0 = pl.multiple_of(ci * cb, cb)
        u_chunk = ubuf[:, pl.ds(c0, cb)]
        for s in range(1, V7X_SUBLANES):
            rbuf[s - 1] = pltpu.roll(u_chunk, rows - s, 0)
        for r0 in range(0, tr, rb):
            acc = jnp.zeros((rb, cb), F32)
            for tap in range(CONV_WIDTH):
                row = halo + r0 + tap - CONV_PAD
                s, base = row % V7X_SUBLANES, row - row % V7X_SUBLANES
                if s == 0:
                    src = ubuf[base:base + rb, pl.ds(c0, cb)]
                else:
                    src = rbuf[s - 1, base:base + rb, :]
                acc = acc + src * w_ref[tap:tap + 1, pl.ds(c0, cb)]
            ybuf[r0:r0 + rb, pl.ds(c0, cb)] = acc + dwb_ref[:, pl.ds(c0, cb)]
        return carry

    lax.fori_loop(0, d // cb, conv_chunk, 0)

    for r0 in range(0, tr, ln_rows):
        y = ybuf[r0:r0 + ln_rows, :]
        mu = jnp.mean(y, axis=-1, keepdims=True)
        yc = y - mu
        var = jnp.mean(yc * yc, axis=-1, keepdims=True)
        z = yc * lax.rsqrt(var + LN_EPS) * lng_ref[...] + lnb_ref[...]
        o_ref[r0:r0 + ln_rows, :] = (z * jax.nn.sigmoid(z)).astype(o_ref.dtype)


def _conv_ln_swish(u, dw_w, dw_b, ln_g, ln_b, *, tr=256, halo=16, rb=64, cb=512):
    b, l, d = u.shape
    assert halo >= CONV_PAD and halo % V7X_SUBLANES == 0 and tr % halo == 0 and l % tr == 0
    assert halo - CONV_PAD >= 0 and CONV_WIDTH - 1 - CONV_PAD <= halo
    hb = tr // halo
    last_halo_block = l // halo - 1
    w_rows = -(-CONV_WIDTH // V7X_SUBLANES) * V7X_SUBLANES
    w_pad = jnp.zeros((w_rows, d), F32).at[:CONV_WIDTH].set(dw_w)
    vec = lambda v: v.reshape(1, d)
    vec_spec = pl.BlockSpec((1, d), lambda bi, ti: (0, 0))
    rows = tr + 2 * halo
    pipelined = (_nbytes((tr, d), F32) + 2 * _nbytes((halo, d), F32) + _nbytes((tr, d), BF16)
                 + _nbytes((w_rows + 3, d), F32))
    resident = (_nbytes((rows, d), F32) + _nbytes((V7X_SUBLANES - 1, rows, cb), F32) + _nbytes((tr, d), F32)
                + 2 * _nbytes((rows, cb), F32) + 8 * _nbytes((rb, d), F32))
    return pl.pallas_call(
        functools.partial(_conv_ln_kernel, halo=halo, rb=rb, cb=cb, ln_rows=min(64, tr)),
        out_shape=jax.ShapeDtypeStruct((b, l, d), BF16),
        grid=(b, l // tr),
        in_specs=[
            pl.BlockSpec((None, halo, d), lambda bi, ti: (bi, jnp.maximum(ti * hb - 1, 0), 0)),
            pl.BlockSpec((None, tr, d), lambda bi, ti: (bi, ti, 0)),
            pl.BlockSpec((None, halo, d), lambda bi, ti: (bi, jnp.minimum((ti + 1) * hb, last_halo_block), 0)),
            pl.BlockSpec((w_rows, d), lambda bi, ti: (0, 0)),
            vec_spec, vec_spec, vec_spec,
        ],
        out_specs=pl.BlockSpec((None, tr, d), lambda bi, ti: (bi, ti, 0)),
        scratch_shapes=[pltpu.VMEM((rows, d), F32),
                        pltpu.VMEM((V7X_SUBLANES - 1, rows, cb), F32),
                        pltpu.VMEM((tr, d), F32)],
        compiler_params=_params(2, _vmem_limit(pipelined, resident)),
        name="conv_ln_swish",
    )(u, u, u, w_pad, vec(dw_b), vec(ln_g), vec(ln_b))


def _rope_tables(length):
    rows = length // GRID_W
    row = jnp.repeat(jnp.arange(rows, dtype=F32), GRID_W)
    col = jnp.tile(jnp.arange(GRID_W, dtype=F32), rows)
    n = DIFF_HEAD_DIM // 4
    inv = ROPE_BASE ** (-jnp.arange(n, dtype=F32) / n)
    ar, ac = row[:, None] * inv, col[:, None] * inv
    ang = jnp.concatenate([ar, ar, ac, ac], axis=-1)
    cos, sin = jnp.cos(ang), jnp.sin(ang)
    first_half = (jnp.arange(DIFF_HEAD_DIM) % (2 * n)) < n
    sin_lo = jnp.where(first_half, -sin, 0.0)
    sin_hi = jnp.where(first_half, 0.0, sin)
    q_scale = ATTN_SCALE * math.log2(math.e)
    return (jnp.stack([cos * q_scale, cos]),
            jnp.stack([sin_lo * q_scale, sin_lo]),
            jnp.stack([sin_hi * q_scale, sin_hi]))


def _mlp(h, w_up, w_down_f32, layer, x, gate, seq_len, *, extra_casts=()):
    n_ff = w_up.shape[1]
    hid, w_down, *hosted = _mm_call(
        _mm_relu2_kernel, h, w_up[None], 0, n_ff, out_dtype=BF16,
        bm=min(MM_BM, seq_len), bn=min(MM_BN, n_ff),
        casts=[(w_down_f32, layer), *extra_casts], name=f"mlp_up_{layer}")
    x_new = _mm_resid(hid, w_down[None], 0, x, gate, seq_len, bk=min(MLP_DOWN_BK, n_ff),
                      name=f"mlp_down_{layer}")
    return x_new, hosted


def kernel(x, c, ctx, c_ctx, ada_w, ada_b, norm_mix_g, norm_mlp_g, norm_final_g, attn_w_qkv, attn_w_o, lambda_q1, lambda_k1, lambda_q2, lambda_k2, attn_subln_g, conv_pw1_w, conv_pw1_b, conv_dw_w, conv_dw_b, conv_ln_g, conv_ln_b, conv_pw2_w, conv_pw2_b, mlp_w1, mlp_w2):
    b, l, d = x.shape
    n_ctx = ctx.shape[1]
    depth = ada_w.shape[0]
    assert depth == 2 and attn_w_qkv.shape[0] == 1 and conv_pw1_w.shape[0] == 1
    n_heads = d // V_HEAD_DIM
    m = b * l

    rows = -(-(b + 1) // V7X_SUBLANES) * V7X_SUBLANES
    cvec = jnp.zeros((rows, d), F32).at[:b].set(c).at[b].set(c_ctx)
    mods = _adaln(cvec, ada_w, ada_b)

    def lat_mod(layer, j):
        return mods[layer, :b, j * d:(j + 1) * d].reshape(b, 1, d)

    def ctx_mod(layer, j):
        return mods[layer, b:b + 1, j * d:(j + 1) * d].reshape(1, 1, d)

    w_qkv = attn_w_qkv.astype(BF16)
    w_pw1 = None

    x_lat = x
    for layer in range(depth):
        kind, idx = layer % N_MIXERS, layer // N_MIXERS
        sh1, sc1, g1, sh2, sc2, g2 = (lat_mod(layer, j) for j in range(6))
        x_flat = x_lat.reshape(m, d)
        bm = min(MM_BM, l)
        if kind == 0:
            lam_init = 0.8 - 0.6 * math.exp(-0.3 * layer)
            h_ctx = _norm(ctx, norm_mix_g[layer], ctx_mod(layer, 0), ctx_mod(layer, 1), out_dtype=BF16)
            bn = min(MM_BN, d)
            qk, v = _qkv_proj(_NormedRows(x_flat, norm_mix_g[layer], sh1, sc1, l), w_qkv, idx, _rope_tables(l),
                              bm=bm, bn_qk=bn)
            kv_ctx = _mm_call(
                _mm_plain_kernel, h_ctx.reshape(b * n_ctx, d), w_qkv, idx, 2 * d,
                w_col_offsets=(d,), out_dtype=BF16, bm=min(MM_BM, b * n_ctx), bn=bn, name="kv_ctx")
            lam_vecs = jnp.stack([lambda_q1[idx], lambda_k1[idx], lambda_q2[idx], lambda_k2[idx]])
            o, w_o, w_up = _diff_attention(
                qk.reshape(b, l, 2 * d), v.reshape(b, l, d), kv_ctx.reshape(b, n_ctx, 2 * d),
                lam_vecs, attn_subln_g[idx], lam_init=lam_init, n_heads=n_heads,
                casts=[(attn_w_o, idx), (mlp_w1, layer)])
            x_flat = _mm_resid(o.reshape(m, d), w_o[None], 0, x_flat, g1, l, name="attn_out")
            mlp_hosted = [(conv_pw1_w, 0)]
        else:
            bn = min(GLU_BN, d)
            bias = conv_pw1_b[idx].reshape(1, 2 * d)
            u, w_pw2, w_up = _mm_call(
                _mm_glu_kernel, _NormedRows(x_flat, norm_mix_g[layer], sh1, sc1, l), w_pw1[None], 0, d,
                w_col_offsets=(0, d),
                extras=[(bias, pl.BlockSpec((1, bn), lambda i, j, k: (0, j))),
                        (bias, pl.BlockSpec((1, bn), lambda i, j, k: (0, j + d // bn)))],
                out_dtype=F32, bm=bm, bn=bn, n_temps=6,
                casts=[(conv_pw2_w, idx), (mlp_w1, layer)], name="conv_pw1_glu")
            s = _conv_ln_swish(u.reshape(b, l, d), conv_dw_w[idx], conv_dw_b[idx],
                               conv_ln_g[idx], conv_ln_b[idx])
            x_flat = _mm_resid(s.reshape(m, d), w_pw2[None], 0, x_flat, g1, l,
                               bias=conv_pw2_b[idx], name="conv_pw2")
            mlp_hosted = []
        h2 = _NormedRows(x_flat, norm_mlp_g[layer], sh2, sc2, l)
        x_flat, hosted = _mlp(h2, w_up, mlp_w2, layer, x_flat, g2, l, extra_casts=mlp_hosted)
        if hosted:
            w_pw1 = hosted[0]
        x_lat = x_flat.reshape(b, l, d)
    return _norm(x_lat, norm_final_g, out_dtype=F32)
```

```python
import functools
import math
from typing import NamedTuple

import jax
import jax.numpy as jnp
from jax import lax
from jax.experimental import pallas as pl
from jax.experimental.pallas import tpu as pltpu

GRID_W = 64
DIFF_HEAD_DIM = 128
V_HEAD_DIM = 2 * DIFF_HEAD_DIM
ATTN_SCALE = DIFF_HEAD_DIM ** -0.5
ROPE_BASE = 10000.0
CONV_WIDTH = 31
CONV_PAD = CONV_WIDTH // 2
RMS_EPS = 1e-6
LN_EPS = 1e-5
N_MIXERS = 2

V7X_LANES = 128
V7X_SUBLANES = 8
V7X_VMEM_LIMIT_CAP = 63 * 1024 * 1024

MM_BM = 1024
MM_BN = 1024
MLP_DOWN_BK = 4096
GLU_BN = 512

F32 = jnp.float32
BF16 = jnp.bfloat16


def _nbytes(shape, dtype):
    return math.prod(shape) * jnp.dtype(dtype).itemsize


def _vmem_limit(pipelined_bytes, resident_bytes):
    return int(min(V7X_VMEM_LIMIT_CAP, 2 * pipelined_bytes + resident_bytes))


def _params(n_axes, vmem_bytes):
    return pltpu.CompilerParams(
        dimension_semantics=("arbitrary",) * n_axes, vmem_limit_bytes=vmem_bytes)


def _adaln_kernel(c_ref, w_ref, b_ref, o_ref):
    c = c_ref[...]
    s = (c * jax.nn.sigmoid(c)).astype(BF16)
    acc = jnp.dot(s, w_ref[...].astype(BF16), preferred_element_type=F32)
    o_ref[...] = acc + b_ref[...]


def _adaln(cvec, ada_w, ada_b, *, bn=512):
    depth, d, n = ada_w.shape
    rows = cvec.shape[0]
    pipelined = _nbytes((d, bn), F32) + _nbytes((rows, bn), F32) + _nbytes((1, bn), F32)
    resident = 2 * _nbytes((rows, d), F32) + _nbytes((d, bn), BF16)
    return pl.pallas_call(
        _adaln_kernel,
        out_shape=jax.ShapeDtypeStruct((depth, rows, n), F32),
        grid=(depth, n // bn),
        in_specs=[
            pl.BlockSpec((rows, d), lambda l, j: (0, 0)),
            pl.BlockSpec((None, d, bn), lambda l, j: (l, 0, j)),
            pl.BlockSpec((None, 1, bn), lambda l, j: (l, 0, j)),
        ],
        out_specs=pl.BlockSpec((None, rows, bn), lambda l, j: (l, 0, j)),
        compiler_params=_params(2, _vmem_limit(pipelined, resident)),
        name="adaln",
    )(cvec, ada_w, ada_b.reshape(depth, 1, n))


def _norm_kernel(x_ref, g_ref, *rest, modulate, rb):
    o_ref = rest[-1]
    x = x_ref[...]
    rstd = lax.rsqrt(jnp.mean(x * x, axis=-1, keepdims=True) + RMS_EPS)
    gain = g_ref[...]
    if modulate:
        sh_ref, sc_ref = rest[0], rest[1]
        gain = gain * (1.0 + sc_ref[...])
    for r0 in range(0, x_ref.shape[0], rb):
        y = (x_ref[r0:r0 + rb, :] * rstd[r0:r0 + rb]) * gain
        if modulate:
            y = y + sh_ref[...]
        o_ref[r0:r0 + rb, :] = y.astype(o_ref.dtype)


def _norm(x, g, shift=None, scale=None, *, out_dtype, tr=512):
    b, t, d = x.shape
    tr = min(tr, t)
    modulate = shift is not None
    operands = [x, g.reshape(1, d)]
    in_specs = [
        pl.BlockSpec((None, tr, d), lambda bi, ti: (bi, ti, 0)),
        pl.BlockSpec((1, d), lambda bi, ti: (0, 0)),
    ]
    if modulate:
        per_batch = shift.shape[0] == b
        vec_map = (lambda bi, ti: (bi, 0, 0)) if per_batch else (lambda bi, ti: (0, 0, 0))
        operands += [shift, scale]
        in_specs += [pl.BlockSpec((None, 1, d), vec_map)] * 2
    pipelined = _nbytes((tr, d), F32) + _nbytes((tr, d), out_dtype) + 3 * _nbytes((1, d), F32)
    resident = 3 * _nbytes((tr, d), F32)
    return pl.pallas_call(
        functools.partial(_norm_kernel, modulate=modulate, rb=min(32, tr)),
        out_shape=jax.ShapeDtypeStruct((b, t, d), out_dtype),
        grid=(b, t // tr),
        in_specs=in_specs,
        out_specs=pl.BlockSpec((None, tr, d), lambda bi, ti: (bi, ti, 0)),
        compiler_params=_params(2, _vmem_limit(pipelined, resident)),
        name="rmsnorm_mod" if modulate else "rmsnorm",
    )(*operands)


def _cast_plan(jobs, grid, step_of):
    n_steps = math.prod(grid)
    in_specs, out_specs, out_shapes, tile_bytes = [], [], [], 0
    for src, layer in jobs:
        _, kdim, n = src.shape
        rows = kdim // n_steps
        assert rows * n_steps == kdim and rows % (2 * V7X_SUBLANES) == 0, (src.shape, n_steps)
        in_specs.append(pl.BlockSpec((None, rows, n), lambda *ids, layer=layer: (layer, step_of(*ids), 0)))
        out_specs.append(pl.BlockSpec((rows, n), lambda *ids: (step_of(*ids), 0)))
        out_shapes.append(jax.ShapeDtypeStruct((kdim, n), BF16))
        tile_bytes += _nbytes((rows, n), F32) + _nbytes((rows, n), BF16)
    return in_specs, out_specs, out_shapes, tile_bytes


def _hosting_casts(body, n_in, n_out, n_casts):
    if n_casts == 0:
        return body

    def wrapped(*refs):
        ins, rest = refs[:n_in], refs[n_in:]
        cast_ins, rest = rest[:n_casts], rest[n_casts:]
        outs, rest = rest[:n_out], rest[n_out:]
        cast_outs, scratch = rest[:n_casts], rest[n_casts:]
        body(*ins, *outs, *scratch)
        for src_ref, dst_ref in zip(cast_ins, cast_outs):
            dst_ref[...] = src_ref[...].astype(dst_ref.dtype)

    return wrapped


class _NormedRows(NamedTuple):
    x: jax.Array
    gain: jax.Array
    shift: jax.Array
    scale: jax.Array
    seq_len: int


def _step_kernel(body, n_in, n_out, n_casts, fused_norm):
    if not fused_norm:
        return _hosting_casts(body, n_in, n_out, n_casts)
    step = _hosting_casts(body, n_in + 1, n_out, n_casts)

    def wrapped(x_ref, g_ref, sh_ref, sc_ref, *refs):
        refs, (h_even, h_odd) = refs[:-2], refs[-2:]
        i, j = pl.program_id(0), pl.program_id(1)
        slab = x_ref.shape[0]

        def normalise_slab_into(dst):
            x = x_ref[...]
            rstd = lax.rsqrt(jnp.mean(x * x, axis=-1, keepdims=True) + RMS_EPS)
            y = (x * rstd) * (g_ref[...] * (1.0 + sc_ref[...])) + sh_ref[...]
            dst[pl.ds(pl.multiple_of(j * slab, slab), slab), :] = y.astype(dst.dtype)

        @pl.when(i == 0)
        def _():
            normalise_slab_into(h_even)

        @pl.when(jnp.logical_and(i > 0, i % 2 == 1))
        def _():
            normalise_slab_into(h_odd)
            step(h_even, *refs)

        @pl.when(jnp.logical_and(i > 0, i % 2 == 0))
        def _():
            normalise_slab_into(h_even)
            step(h_odd, *refs)

    return wrapped


def _dot(a_ref, w_ref):
    return jnp.dot(a_ref[...], w_ref[...], preferred_element_type=F32)


def _mm_plain_kernel(a_ref, w_ref, o_ref):
    o_ref[...] = _dot(a_ref, w_ref).astype(o_ref.dtype)


def _qkv_kernel(a_ref, wqk_ref, wv_ref, cos_ref, sin_lo_ref, sin_hi_ref, qk_ref, v_ref):
    a = a_ref[...]
    acc = jnp.dot(a, wqk_ref[...], preferred_element_type=F32)
    cos, sin_lo, sin_hi = cos_ref[...], sin_lo_ref[...], sin_hi_ref[...]
    quarter = DIFF_HEAD_DIM // 4
    for c0 in range(0, acc.shape[1], DIFF_HEAD_DIM):
        xh = acc[:, c0:c0 + DIFF_HEAD_DIM]
        r = (xh * cos
             + pltpu.roll(xh, DIFF_HEAD_DIM - quarter, 1) * sin_lo
             + pltpu.roll(xh, quarter, 1) * sin_hi)
        qk_ref[:, c0:c0 + DIFF_HEAD_DIM] = r.astype(qk_ref.dtype)
    v_ref[...] = jnp.dot(a, wv_ref[...], preferred_element_type=F32).astype(v_ref.dtype)


def _qkv_proj(a, w_qkv, layer, tables, *, bm, bn_qk):
    m, d = a.x.shape
    n_i, n_j = m // bm, 2 * d // bn_qk
    bn_v = d // n_j
    assert m % bm == 0 and a.seq_len % bm == 0 and d % bn_qk == 0 and bn_v % V7X_LANES == 0 and bm % n_j == 0
    slab, last = bm // n_j, n_i - 1
    blocks_per_batch = a.seq_len // bm
    row = lambda i: jnp.maximum(i - 1, 0)
    col = lambda i, j: jnp.where(i == 0, 0, j)
    tab_spec = pl.BlockSpec((None, bm, DIFF_HEAD_DIM),
                            lambda i, j: (col(i, j) // (d // bn_qk), row(i) % blocks_per_batch, 0))
    vec_spec = pl.BlockSpec((None, 1, d), lambda i, j: (jnp.minimum(i, last) // blocks_per_batch, 0, 0))
    pipelined = (_nbytes((d, bn_qk + bn_v), BF16) + _nbytes((bm, bn_qk + bn_v), BF16)
                 + 3 * _nbytes((bm, DIFF_HEAD_DIM), F32) + _nbytes((slab, d), F32) + 3 * _nbytes((1, d), F32))
    resident = 3 * _nbytes((bm, bn_qk), F32) + 2 * _nbytes((bm, d), BF16) + 3 * _nbytes((slab, d), F32)
    return pl.pallas_call(
        _step_kernel(_qkv_kernel, 5, 2, 0, True),
        out_shape=(jax.ShapeDtypeStruct((m, 2 * d), BF16), jax.ShapeDtypeStruct((m, d), BF16)),
        grid=(n_i + 1, n_j),
        in_specs=[
            pl.BlockSpec((slab, d), lambda i, j: (jnp.minimum(i, last) * n_j + j, 0)),
            pl.BlockSpec((1, d), lambda i, j: (0, 0)), vec_spec, vec_spec,
            pl.BlockSpec((None, d, bn_qk), lambda i, j: (layer, 0, col(i, j))),
            pl.BlockSpec((None, d, bn_v), lambda i, j: (layer, 0, 2 * d // bn_v + col(i, j))),
            tab_spec, tab_spec, tab_spec,
        ],
        out_specs=(pl.BlockSpec((bm, bn_qk), lambda i, j: (row(i), col(i, j))),
                   pl.BlockSpec((bm, bn_v), lambda i, j: (row(i), col(i, j)))),
        scratch_shapes=[pltpu.VMEM((bm, d), BF16)] * 2,
        compiler_params=_params(2, _vmem_limit(pipelined, resident)),
        name="qkv_rope",
    )(a.x, a.gain.reshape(1, d), a.shift, a.scale, w_qkv, w_qkv, *tables)


def _mm_relu2_kernel(a_ref, w_ref, o_ref):
    acc = _dot(a_ref, w_ref)
    r = jnp.maximum(acc, 0.0)
    o_ref[...] = (r * r).astype(o_ref.dtype)


def _mm_glu_kernel(a_ref, wa_ref, wg_ref, ba_ref, bg_ref, o_ref):
    a = a_ref[...]
    half = o_ref.shape[1] // 2
    for c0 in (0, half):
        lin = jnp.dot(a, wa_ref[:, c0:c0 + half], preferred_element_type=F32) + ba_ref[:, c0:c0 + half]
        gate = jnp.dot(a, wg_ref[:, c0:c0 + half], preferred_element_type=F32) + bg_ref[:, c0:c0 + half]
        o_ref[:, c0:c0 + half] = (lin * jax.nn.sigmoid(gate)).astype(o_ref.dtype)


def _mm_resid_kernel(a_ref, w_ref, x_ref, gate_ref, *rest, nk, has_bias):
    bias_ref, o_ref = (rest[0], rest[1]) if has_bias else (None, rest[0])
    if nk == 1:
        y = _dot(a_ref, w_ref)
        if has_bias:
            y = y + bias_ref[...]
        o_ref[...] = x_ref[...] + gate_ref[...] * y
        return

    @pl.when(pl.program_id(2) == 0)
    def _():
        x0 = x_ref[...]
        o_ref[...] = x0 + gate_ref[...] * bias_ref[...] if has_bias else x0

    o_ref[...] += gate_ref[...] * _dot(a_ref, w_ref)


def _mm_call(body, a, w, layer, n_cols, *, w_col_offsets=(0,), extras=(), out_dtype, bm, bn, bk=None,
             extra_tile_bytes=0, n_temps=3, casts=(), name):
    fused = isinstance(a, _NormedRows)
    m, kdim = a.x.shape if fused else a.shape
    bk = kdim if bk is None else bk
    nk = kdim // bk
    assert m % bm == 0 and n_cols % bn == 0 and kdim % bk == 0, (name, m, kdim, n_cols, bm, bn, bk)
    n_i, n_j = m // bm, n_cols // bn
    if fused:
        assert nk == 1 and bm % n_j == 0 and a.seq_len % bm == 0
        grid = (n_i + 1, n_j, 1)
        row = lambda i: jnp.maximum(i - 1, 0)
        col = lambda i, j: jnp.where(i == 0, 0, j)
    else:
        grid = (n_i, n_j, nk)
        row = lambda i: i
        col = lambda i, j: j

    def remap(index_map):
        return lambda i, j, k: index_map(row(i), col(i, j), k)

    in_specs = [] if fused else [pl.BlockSpec((bm, bk), remap(lambda i, j, k: (i, k)))]
    for off in w_col_offsets:
        assert off % bn == 0
        in_specs.append(pl.BlockSpec((None, bk, bn), remap(lambda i, j, k, ob=off // bn: (layer, k, j + ob))))
    in_specs += [pl.BlockSpec(spec.block_shape, remap(spec.index_map)) for _, spec in extras]
    cast_in, cast_out, cast_shapes, cast_bytes = _cast_plan(
        casts, (n_i, n_j, nk), lambda i, j, k: (row(i) * n_j + col(i, j)) * nk + k)
    pipelined = (len(w_col_offsets) * _nbytes((bk, bn), BF16) + _nbytes((bm, bn), out_dtype)
                 + extra_tile_bytes + cast_bytes)
    resident = n_temps * _nbytes((bm, bn), F32)
    operands, norm_specs, scratch = [], [], []
    if fused:
        slab, last = bm // n_j, n_i - 1
        blocks_per_batch = a.seq_len // bm
        vec_spec = pl.BlockSpec((None, 1, kdim), lambda i, j, k: (jnp.minimum(i, last) // blocks_per_batch, 0, 0))
        operands = [a.x, a.gain.reshape(1, kdim), a.shift, a.scale]
        norm_specs = [pl.BlockSpec((slab, kdim), lambda i, j, k: (jnp.minimum(i, last) * n_j + j, 0)),
                      pl.BlockSpec((1, kdim), lambda i, j, k: (0, 0)), vec_spec, vec_spec]
        scratch = [pltpu.VMEM((bm, kdim), BF16)] * 2
        pipelined += _nbytes((slab, kdim), F32) + 3 * _nbytes((1, kdim), F32)
        resident += 2 * _nbytes((bm, kdim), BF16) + 3 * _nbytes((slab, kdim), F32)
    else:
        operands = [a]
        pipelined += _nbytes((bm, bk), BF16)
    outs = pl.pallas_call(
        _step_kernel(body, len(in_specs), 1, len(casts), fused),
        out_shape=[jax.ShapeDtypeStruct((m, n_cols), out_dtype)] + cast_shapes,
        grid=grid,
        in_specs=norm_specs + in_specs + cast_in,
        out_specs=[pl.BlockSpec((bm, bn), remap(lambda i, j, k: (i, j)))] + cast_out,
        scratch_shapes=scratch,
        compiler_params=_params(3, _vmem_limit(pipelined, resident)),
        name=name,
    )(*operands, *([w] * len(w_col_offsets)), *[arr for arr, _ in extras], *[src for src, _ in casts])
    return outs[0] if not casts else tuple(outs)


def _mm_resid(a, w, layer, x, gate, seq_len, *, bias=None, bk=None, name):
    n = w.shape[2]
    bm, bn = min(MM_BM, seq_len), min(MM_BN, n)
    bk = a.shape[1] if bk is None else bk
    nk = a.shape[1] // bk
    blocks_per_batch = seq_len // bm
    extras = [
        (x, pl.BlockSpec((bm, bn), lambda i, j, k: (i, j))),
        (gate, pl.BlockSpec((None, 1, bn), lambda i, j, k: (i // blocks_per_batch, 0, j))),
    ]
    if bias is not None:
        extras.append((bias.reshape(1, n), pl.BlockSpec((1, bn), lambda i, j, k: (0, j))))
    return _mm_call(
        functools.partial(_mm_resid_kernel, nk=nk, has_bias=bias is not None),
        a, w, layer, n, extras=extras, out_dtype=F32, bm=bm, bn=bn, bk=bk,
        extra_tile_bytes=_nbytes((bm, bn), F32), n_temps=2, name=name)


def _lane_fold_sum(x):
    out = x[:, 0:V7X_LANES]
    for c0 in range(V7X_LANES, x.shape[1], V7X_LANES):
        out = out + x[:, c0:c0 + V7X_LANES]
    return out


def _attn_kernel(lam_vecs_ref, subln_g_ref, q_ref, kc_ref, k_ref, vc_ref, v_ref, o_ref,
                 lam_ref, *, lam_init, key_chunk, q_sub):
    dk = DIFF_HEAD_DIM
    n_ctx, n_lat = kc_ref.shape[0], k_ref.shape[0]

    first = jnp.logical_and(pl.program_id(0) == 0,
                            jnp.logical_and(pl.program_id(1) == 0, pl.program_id(2) == 0))

    @pl.when(first)
    def _():
        lv = lam_vecs_ref[...]
        d1 = jnp.sum(lv[0:1, :] * lv[1:2, :], axis=-1, keepdims=True)
        d2 = jnp.sum(lv[2:3, :] * lv[3:4, :], axis=-1, keepdims=True)
        lam_ref[...] = jnp.broadcast_to(jnp.exp(d1) - jnp.exp(d2) + lam_init, lam_ref.shape)

    groups = [(kc_ref, vc_ref, 0, n_ctx)]
    groups += [(k_ref, v_ref, r0, key_chunk) for r0 in range(0, n_lat, key_chunk)]

    lam = lam_ref[0:1, 0:1]
    hw = V_HEAD_DIM
    for h0 in range(0, q_ref.shape[1], hw):
        for q0 in range(0, q_ref.shape[0], q_sub):
            heads = []
            for comp in range(2):
                c0 = h0 + comp * dk
                qc = q_ref[q0:q0 + q_sub, c0:c0 + dk]
                m = lsum = acc = None
                for key_ref, val_ref, r0, rows in groups:
                    kk = key_ref[r0:r0 + rows, c0:c0 + dk]
                    s = lax.dot_general(qc, kk, (((1,), (1,)), ((), ())), preferred_element_type=F32)
                    gmax = jnp.max(s, axis=-1, keepdims=True)
                    m_new = gmax if m is None else jnp.maximum(m, gmax)
                    e = jnp.exp2(s - m_new)
                    esum = _lane_fold_sum(e)
                    pv = jnp.dot(e.astype(BF16), val_ref[r0:r0 + rows, h0:h0 + hw], preferred_element_type=F32)
                    if m is None:
                        lsum, acc = esum, pv
                    else:
                        alpha = jnp.exp2(m - m_new)
                        lsum, acc = lsum * alpha + esum, acc * alpha + pv
                    m = m_new
                heads.append(acc / jnp.sum(lsum, axis=-1, keepdims=True))

            o = heads[0] - lam * heads[1]
            o = o * lax.rsqrt(jnp.mean(o * o, axis=-1, keepdims=True) + RMS_EPS)
            o = (o * subln_g_ref[...]) * (1.0 - lam_init)
            o_ref[q0:q0 + q_sub, h0:h0 + hw] = o.astype(o_ref.dtype)


def _diff_attention(qk, v, kv_ctx, lam_vecs, subln_g, *, lam_init, n_heads, bq=2048, q_sub=512, key_chunk=1024,
                    heads_per_step=1, casts=()):
    b, l, d = v.shape
    c = kv_ctx.shape[1]
    bq, key_chunk = min(bq, l), min(key_chunk, l)
    q_sub = min(q_sub, bq)
    hw = V_HEAD_DIM
    assert hw == 2 * DIFF_HEAD_DIM and n_heads * hw == d and l % key_chunk == 0 and n_heads % heads_per_step == 0
    bw = heads_per_step * hw
    hb = d // bw
    grid = (b, n_heads // heads_per_step, l // bq)
    cast_in, cast_out, cast_shapes, cast_bytes = _cast_plan(
        casts, grid, lambda bi, h, qi: (bi * grid[1] + h) * grid[2] + qi)
    pipelined = (2 * _nbytes((bq, bw), BF16) + 2 * _nbytes((c, bw), BF16) + 2 * _nbytes((l, bw), BF16)
                 + cast_bytes)
    resident = (heads_per_step * (bq // q_sub)
                * (4 * _nbytes((q_sub, key_chunk), F32) + 8 * _nbytes((q_sub, hw), F32)))
    in_specs = [
        pl.BlockSpec((4, DIFF_HEAD_DIM), lambda bi, h, qi: (0, 0)),
        pl.BlockSpec((1, hw), lambda bi, h, qi: (0, 0)),
        pl.BlockSpec((None, bq, bw), lambda bi, h, qi: (bi, qi, h)),
        pl.BlockSpec((None, c, bw), lambda bi, h, qi: (bi, 0, h)),
        pl.BlockSpec((None, l, bw), lambda bi, h, qi: (bi, 0, hb + h)),
        pl.BlockSpec((None, c, bw), lambda bi, h, qi: (bi, 0, hb + h)),
        pl.BlockSpec((None, l, bw), lambda bi, h, qi: (bi, 0, h)),
    ]
    outs = pl.pallas_call(
        _hosting_casts(functools.partial(_attn_kernel, lam_init=lam_init, key_chunk=key_chunk, q_sub=q_sub),
                       len(in_specs), 1, len(casts)),
        out_shape=[jax.ShapeDtypeStruct((b, l, d), BF16)] + cast_shapes,
        grid=grid,
        in_specs=in_specs + cast_in,
        out_specs=[pl.BlockSpec((None, bq, bw), lambda bi, h, qi: (bi, qi, h))] + cast_out,
        scratch_shapes=[pltpu.VMEM((V7X_SUBLANES, V7X_LANES), F32)],
        compiler_params=_params(3, _vmem_limit(pipelined, resident)),
        name="diff_attention",
    )(lam_vecs, subln_g.reshape(1, hw), qk, kv_ctx, qk, kv_ctx, v, *[src for src, _ in casts])
    return outs[0] if not casts else tuple(outs)


def _conv_ln_kernel(prev_ref, u_ref, next_ref, w_ref, dwb_ref, lng_ref, lnb_ref, o_ref,
                    ubuf, rbuf, ybuf, *, halo, rb, cb, ln_rows):
    tr, d = u_ref.shape
    rows = tr + 2 * halo
    ti, nt = pl.program_id(1), pl.num_programs(1)
    ubuf[0:halo, :] = jnp.where(ti > 0, prev_ref[...], 0.0)
    ubuf[halo:halo + tr, :] = u_ref[...]
    ubuf[halo + tr:, :] = jnp.where(ti < nt - 1, next_ref[...], 0.0)

    def conv_chunk(ci, carry):
        c0 = pl.multiple_of(ci * cb, cb)
        u_chunk = ubuf[:, pl.ds(c0, cb)]
        for s in range(1, V7X_SUBLANES):
            rbuf[s - 1] = pltpu.roll(u_chunk, rows - s, 0)
        w_chunk = w_ref[:, pl.ds(c0, cb)]
        w_tiles = [jnp.broadcast_to(w_chunk[tap:tap + 1], (V7X_SUBLANES, cb)) for tap in range(CONV_WIDTH)]
        for r0 in range(0, tr, rb):
            acc = jnp.zeros((rb // V7X_SUBLANES, V7X_SUBLANES, cb), F32)
            for tap in range(CONV_WIDTH):
                row = halo + r0 + tap - CONV_PAD
                s, base = row % V7X_SUBLANES, row - row % V7X_SUBLANES
                if s == 0:
                    src = ubuf[base:base + rb, pl.ds(c0, cb)]
                else:
                    src = rbuf[s - 1, base:base + rb, :]
                acc = acc + src.reshape(rb // V7X_SUBLANES, V7X_SUBLANES, cb) * w_tiles[tap][None]
            ybuf[r0:r0 + rb, pl.ds(c0, cb)] = acc.reshape(rb, cb) + dwb_ref[:, pl.ds(c0, cb)]
        return carry

    lax.fori_loop(0, d // cb, conv_chunk, 0)

    for r0 in range(0, tr, ln_rows):
        y = ybuf[r0:r0 + ln_rows, :]
        mu = jnp.mean(y, axis=-1, keepdims=True)
        yc = y - mu
        var = jnp.mean(yc * yc, axis=-1, keepdims=True)
        z = yc * lax.rsqrt(var + LN_EPS) * lng_ref[...] + lnb_ref[...]
        o_ref[r0:r0 + ln_rows, :] = (z * jax.nn.sigmoid(z)).astype(o_ref.dtype)


def _conv_ln_swish(u, dw_w, dw_b, ln_g, ln_b, *, tr=256, halo=16, rb=64, cb=512):
    b, l, d = u.shape
    assert halo >= CONV_PAD and halo % V7X_SUBLANES == 0 and tr % halo == 0 and l % tr == 0
    assert halo - CONV_PAD >= 0 and CONV_WIDTH - 1 - CONV_PAD <= halo
    hb = tr // halo
    last_halo_block = l // halo - 1
    w_rows = -(-CONV_WIDTH // V7X_SUBLANES) * V7X_SUBLANES
    w_pad = jnp.zeros((w_rows, d), F32).at[:CONV_WIDTH].set(dw_w)
    vec = lambda v: v.reshape(1, d)
    vec_spec = pl.BlockSpec((1, d), lambda bi, ti: (0, 0))
    rows = tr + 2 * halo
    pipelined = (_nbytes((tr, d), F32) + 2 * _nbytes((halo, d), F32) + _nbytes((tr, d), BF16)
                 + _nbytes((w_rows + 3, d), F32))
    resident = (_nbytes((rows, d), F32) + _nbytes((V7X_SUBLANES - 1, rows, cb), F32) + _nbytes((tr, d), F32)
                + 2 * _nbytes((rows, cb), F32) + 8 * _nbytes((rb, d), F32))
    return pl.pallas_call(
        functools.partial(_conv_ln_kernel, halo=halo, rb=rb, cb=cb, ln_rows=min(64, tr)),
        out_shape=jax.ShapeDtypeStruct((b, l, d), BF16),
        grid=(b, l // tr),
        in_specs=[
            pl.BlockSpec((None, halo, d), lambda bi, ti: (bi, jnp.maximum(ti * hb - 1, 0), 0)),
            pl.BlockSpec((None, tr, d), lambda bi, ti: (bi, ti, 0)),
            pl.BlockSpec((None, halo, d), lambda bi, ti: (bi, jnp.minimum((ti + 1) * hb, last_halo_block), 0)),
            pl.BlockSpec((w_rows, d), lambda bi, ti: (0, 0)),
            vec_spec, vec_spec, vec_spec,
        ],
        out_specs=pl.BlockSpec((None, tr, d), lambda bi, ti: (bi, ti, 0)),
        scratch_shapes=[pltpu.VMEM((rows, d), F32),
                        pltpu.VMEM((V7X_SUBLANES - 1, rows, cb), F32),
                        pltpu.VMEM((tr, d), F32)],
        compiler_params=_params(2, _vmem_limit(pipelined, resident)),
        name="conv_ln_swish",
    )(u, u, u, w_pad, vec(dw_b), vec(ln_g), vec(ln_b))


def _rope_tables(length):
    rows = length // GRID_W
    row = jnp.repeat(jnp.arange(rows, dtype=F32), GRID_W)
    col = jnp.tile(jnp.arange(GRID_W, dtype=F32), rows)
    n = DIFF_HEAD_DIM // 4
    inv = ROPE_BASE ** (-jnp.arange(n, dtype=F32) / n)
    ar, ac = row[:, None] * inv, col[:, None] * inv
    ang = jnp.concatenate([ar, ar, ac, ac], axis=-1)
    cos, sin = jnp.cos(ang), jnp.sin(ang)
    first_half = (jnp.arange(DIFF_HEAD_DIM) % (2 * n)) < n
    sin_lo = jnp.where(first_half, -sin, 0.0)
    sin_hi = jnp.where(first_half, 0.0, sin)
    q_scale = ATTN_SCALE * math.log2(math.e)
    return (jnp.stack([cos * q_scale, cos]),
            jnp.stack([sin_lo * q_scale, sin_lo]),
            jnp.stack([sin_hi * q_scale, sin_hi]))


def _mlp(h, w_up, w_down_f32, layer, x, gate, seq_len, *, extra_casts=()):
    n_ff = w_up.shape[1]
    hid, w_down, *hosted = _mm_call(
        _mm_relu2_kernel, h, w_up[None], 0, n_ff, out_dtype=BF16,
        bm=min(MM_BM, seq_len), bn=min(MM_BN, n_ff),
        casts=[(w_down_f32, layer), *extra_casts], name=f"mlp_up_{layer}")
    x_new = _mm_resid(hid, w_down[None], 0, x, gate, seq_len, bk=min(MLP_DOWN_BK, n_ff),
                      name=f"mlp_down_{layer}")
    return x_new, hosted


def kernel(x, c, ctx, c_ctx, ada_w, ada_b, norm_mix_g, norm_mlp_g, norm_final_g, attn_w_qkv, attn_w_o, lambda_q1, lambda_k1, lambda_q2, lambda_k2, attn_subln_g, conv_pw1_w, conv_pw1_b, conv_dw_w, conv_dw_b, conv_ln_g, conv_ln_b, conv_pw2_w, conv_pw2_b, mlp_w1, mlp_w2):
    b, l, d = x.shape
    n_ctx = ctx.shape[1]
    depth = ada_w.shape[0]
    assert depth == 2 and attn_w_qkv.shape[0] == 1 and conv_pw1_w.shape[0] == 1
    n_heads = d // V_HEAD_DIM
    m = b * l

    rows = -(-(b + 1) // V7X_SUBLANES) * V7X_SUBLANES
    cvec = jnp.zeros((rows, d), F32).at[:b].set(c).at[b].set(c_ctx)
    mods = _adaln(cvec, ada_w, ada_b)

    def lat_mod(layer, j):
        return mods[layer, :b, j * d:(j + 1) * d].reshape(b, 1, d)

    def ctx_mod(layer, j):
        return mods[layer, b:b + 1, j * d:(j + 1) * d].reshape(1, 1, d)

    w_qkv = attn_w_qkv.astype(BF16)
    w_pw1 = None

    x_lat = x
    for layer in range(depth):
        kind, idx = layer % N_MIXERS, layer // N_MIXERS
        sh1, sc1, g1, sh2, sc2, g2 = (lat_mod(layer, j) for j in range(6))
        x_flat = x_lat.reshape(m, d)
        bm = min(MM_BM, l)
        if kind == 0:
            lam_init = 0.8 - 0.6 * math.exp(-0.3 * layer)
            h_ctx = _norm(ctx, norm_mix_g[layer], ctx_mod(layer, 0), ctx_mod(layer, 1), out_dtype=BF16)
            bn = min(MM_BN, d)
            qk, v = _qkv_proj(_NormedRows(x_flat, norm_mix_g[layer], sh1, sc1, l), w_qkv, idx, _rope_tables(l),
                              bm=bm, bn_qk=bn)
            kv_ctx = _mm_call(
                _mm_plain_kernel, h_ctx.reshape(b * n_ctx, d), w_qkv, idx, 2 * d,
                w_col_offsets=(d,), out_dtype=BF16, bm=min(MM_BM, b * n_ctx), bn=bn, name="kv_ctx")
            lam_vecs = jnp.stack([lambda_q1[idx], lambda_k1[idx], lambda_q2[idx], lambda_k2[idx]])
            o, w_o, w_up = _diff_attention(
                qk.reshape(b, l, 2 * d), v.reshape(b, l, d), kv_ctx.reshape(b, n_ctx, 2 * d),
                lam_vecs, attn_subln_g[idx], lam_init=lam_init, n_heads=n_heads,
                casts=[(attn_w_o, idx), (mlp_w1, layer)])
            x_flat = _mm_resid(o.reshape(m, d), w_o[None], 0, x_flat, g1, l, name="attn_out")
            mlp_hosted = [(conv_pw1_w, 0)]
        else:
            bn = min(GLU_BN, d)
            bias = conv_pw1_b[idx].reshape(1, 2 * d)
            u, w_pw2, w_up = _mm_call(
                _mm_glu_kernel, _NormedRows(x_flat, norm_mix_g[layer], sh1, sc1, l), w_pw1[None], 0, d,
                w_col_offsets=(0, d),
                extras=[(bias, pl.BlockSpec((1, bn), lambda i, j, k: (0, j))),
                        (bias, pl.BlockSpec((1, bn), lambda i, j, k: (0, j + d // bn)))],
                out_dtype=F32, bm=bm, bn=bn, n_temps=6,
                casts=[(conv_pw2_w, idx), (mlp_w1, layer)], name="conv_pw1_glu")
            s = _conv_ln_swish(u.reshape(b, l, d), conv_dw_w[idx], conv_dw_b[idx],
                               conv_ln_g[idx], conv_ln_b[idx])
            x_flat = _mm_resid(s.reshape(m, d), w_pw2[None], 0, x_flat, g1, l,
                               bias=conv_pw2_b[idx], name="conv_pw2")
            mlp_hosted = []
        h2 = _NormedRows(x_flat, norm_mlp_g[layer], sh2, sc2, l)
        x_flat, hosted = _mlp(h2, w_up, mlp_w2, layer, x_flat, g2, l, extra_casts=mlp_hosted)
        if hosted:
            w_pw1 = hosted[0]
        x_lat = x_flat.reshape(b, l, d)
    return _norm(x_lat, norm_final_g, out_dtype=F32)
```
